```python
import math
import jax, jax.numpy as jnp
from jax import lax
import numpy as np

D_MODEL = 1024
BATCH = 4
SEQ = 4096
DEPTH = 1
DEC_BATCH = 128
DEC_SEQ = 8
PAST_LEN = 8192
PAGE_SIZE = 128

HEAD_DIM = 64
MLA_HEADS = D_MODEL // 2 // HEAD_DIM
MLA_NOPE_DIM = HEAD_DIM
MLA_ROPE_DIM = HEAD_DIM // 2
MLA_V_DIM = HEAD_DIM
Q_LORA = 3 * D_MODEL // 8
KV_LORA = 4 * HEAD_DIM
SB_HEADS = D_MODEL // 2 // HEAD_DIM
SB_DIM = HEAD_DIM
MLA_WIDTH = MLA_HEADS * MLA_V_DIM
SB_WIDTH = SB_HEADS * SB_DIM
IN_WIDTH = Q_LORA + KV_LORA + MLA_ROPE_DIM + 3 * SB_WIDTH
MLA_SCALE = 1.0 / math.sqrt(MLA_NOPE_DIM + MLA_ROPE_DIM)
SB_SCALE = 1.0 / math.sqrt(SB_DIM)
ROPE_THETA = 10000.0
N_EXPERTS = 32
TOP_K = 4
D_FF = D_MODEL
SWIGLU_ALPHA = 1.702
SWIGLU_LIMIT = 7.0
Q_BLOCK = 128
MOE_BLOCK = 128
EPS = 1e-6

kernel_name = "hybrid_mla_stickbreaking_moe_step"


def rmsnorm(x, g):
    xf = x.astype(jnp.float32)
    y = xf * lax.rsqrt(jnp.mean(xf * xf, axis=-1, keepdims=True) + EPS)
    return (y * g.astype(jnp.float32)).astype(x.dtype)


def rope(x, pos):
    half = x.shape[-1] // 2
    inv = ROPE_THETA ** (-jnp.arange(half, dtype=jnp.float32) / half)
    ang = pos.astype(jnp.float32)[:, None] * inv[None, :]
    cos = jnp.cos(ang)[:, None, :]
    sin = jnp.sin(ang)[:, None, :]
    x1 = x[..., :half].astype(jnp.float32)
    x2 = x[..., half:].astype(jnp.float32)
    return jnp.concatenate([x1 * cos - x2 * sin, x2 * cos + x1 * sin], axis=-1).astype(x.dtype)


def mixer_inputs(xn, pos, w_in, g_q_norm, w_q_up, g_kv_norm, w_uk):
    b, t, _ = xn.shape
    proj = jnp.einsum('btd,de->bte', xn, w_in)
    o1 = Q_LORA
    o2 = o1 + KV_LORA
    o3 = o2 + MLA_ROPE_DIM
    o4 = o3 + SB_WIDTH
    o5 = o4 + SB_WIDTH
    cq = proj[..., :o1]
    ckv = proj[..., o1:o2]
    kr = proj[..., o2:o3]
    sq = proj[..., o3:o4].reshape(b, t, SB_HEADS, SB_DIM)
    sk = proj[..., o4:o5].reshape(b, t, SB_HEADS, SB_DIM)
    sv = proj[..., o5:].reshape(b, t, SB_HEADS, SB_DIM)
    q = jnp.einsum('btc,che->bthe', rmsnorm(cq, g_q_norm), w_q_up)
    q_abs = jnp.einsum('bthn,rhn->bthr', q[..., :MLA_NOPE_DIM], w_uk)
    q_rope = rope(q[..., MLA_NOPE_DIM:], pos)
    c_kv = rmsnorm(ckv, g_kv_norm)
    k_rope = rope(kr[:, :, None, :], pos)[:, :, 0, :]
    return q_abs, q_rope, c_kv, k_rope, sq, sk, sv


def mla_core(q_abs, q_rope, c_kv, k_rope, q_pos, k_pos):
    s = (jnp.einsum('bthr,blr->bhtl', q_abs, c_kv)
         + jnp.einsum('bthd,bld->bhtl', q_rope, k_rope)).astype(jnp.float32) * MLA_SCALE
    mask = k_pos[None, :] <= q_pos[:, None]
    p = jax.nn.softmax(jnp.where(mask[None, None], s, -jnp.inf), axis=-1)
    return jnp.einsum('bhtl,blr->bthr', p.astype(c_kv.dtype), c_kv)


def sb_core(q, k, v, q_pos, k_pos):
    z = jnp.einsum('bthd,blhd->bhtl', q, k).astype(jnp.float32) * SB_SCALE
    mask = (k_pos[None, :] < q_pos[:, None])[None, None]
    l = jnp.where(mask, jax.nn.log_sigmoid(-z), 0.0)
    suffix = lax.cumsum(l, axis=3, reverse=True) - l
    a = jnp.where(mask, jnp.exp(jax.nn.log_sigmoid(z) + suffix), 0.0)
    return jnp.einsum('bhtl,blhd->bthd', a.astype(v.dtype), v)


def prompt_attention(q_abs, q_rope, c_kv, k_rope, sq, sk, sv, pos):
    b, s = q_abs.shape[:2]
    nb = s // Q_BLOCK

    def blocks(a):
        return a.reshape((b, nb, Q_BLOCK) + a.shape[2:]).swapaxes(0, 1)

    def unblock(a):
        return a.swapaxes(0, 1).reshape((b, s) + a.shape[3:])

    def one_block(args):
        qa_b, qr_b, sq_b, qpos_b = args
        return (mla_core(qa_b, qr_b, c_kv, k_rope, qpos_b, pos),
                sb_core(sq_b, sk, sv, qpos_b, pos))

    o_lat, o_sb = lax.map(one_block, (blocks(q_abs), blocks(q_rope), blocks(sq), pos.reshape(nb, Q_BLOCK)))
    return unblock(o_lat), unblock(o_sb)


def sample_attention(q_abs, q_rope, c_new, kr_new, sq, sk_new, sv_new,
                     cache_ckv, cache_krope, cache_sb_k, cache_sb_v, page_table, layer, past_len):
    t = q_abs.shape[1]
    k_pos = jnp.arange(past_len + t)
    q_pos = past_len + jnp.arange(t)

    def one_seq(args):
        pt, qa_b, qr_b, c_b, kr_b, sq_b, sk_b, sv_b = args
        c_all = jnp.concatenate([cache_ckv[layer, pt].reshape(past_len, KV_LORA), c_b], axis=0)
        kr_all = jnp.concatenate([cache_krope[layer, pt].reshape(past_len, MLA_ROPE_DIM), kr_b], axis=0)
        k_all = jnp.concatenate([cache_sb_k[layer, pt].reshape(past_len, SB_HEADS, SB_DIM), sk_b], axis=0)
        v_all = jnp.concatenate([cache_sb_v[layer, pt].reshape(past_len, SB_HEADS, SB_DIM), sv_b], axis=0)
        o_lat = mla_core(qa_b[None], qr_b[None], c_all[None], kr_all[None], q_pos, k_pos)[0]
        o_sb = sb_core(sq_b[None], k_all[None], v_all[None], q_pos, k_pos)[0]
        return o_lat, o_sb

    return lax.map(one_seq, (page_table, q_abs, q_rope, c_new, kr_new, sq, sk_new, sv_new))


def mixer_output(o_lat, o_sb, w_uv, g_mla_out, g_sb_out, w_out):
    b, t = o_lat.shape[:2]
    o_mla = jnp.einsum('bthr,rhv->bthv', o_lat, w_uv).reshape(b, t, MLA_WIDTH)
    o = jnp.concatenate([rmsnorm(o_mla, g_mla_out), rmsnorm(o_sb.reshape(b, t, SB_WIDTH), g_sb_out)], axis=-1)
    return jnp.einsum('bte,ed->btd', o, w_out)


def moe(x_tok, w_router, b_router, w_up, b_up, w_down, b_down):
    n_tok = x_tok.shape[0]
    n_pairs = n_tok * TOP_K
    logits = (x_tok @ w_router + b_router).astype(jnp.float32)
    top_val, top_idx = lax.top_k(logits, TOP_K)
    gate = jax.nn.softmax(top_val, axis=-1)
    e_flat = top_idx.reshape(n_pairs).astype(jnp.int32)
    tok_flat = jnp.repeat(jnp.arange(n_tok, dtype=jnp.int32), TOP_K)
    w_flat = gate.reshape(n_pairs)
    order = jnp.argsort(e_flat)
    e_s, tok_s, w_s = e_flat[order], tok_flat[order], w_flat[order]
    counts = jnp.bincount(e_flat, length=N_EXPERTS)
    starts = jnp.cumsum(counts) - counts
    pcounts = (counts + MOE_BLOCK - 1) // MOE_BLOCK * MOE_BLOCK
    pends = jnp.cumsum(pcounts)
    pstarts = pends - pcounts
    dest = pstarts[e_s] + jnp.arange(n_pairs, dtype=jnp.int32) - starts[e_s]
    n_blocks = (n_pairs + N_EXPERTS * (MOE_BLOCK - 1) + MOE_BLOCK - 1) // MOE_BLOCK
    tok_buf = jnp.zeros((n_blocks * MOE_BLOCK,), jnp.int32).at[dest].set(tok_s)
    w_buf = jnp.zeros((n_blocks * MOE_BLOCK,), jnp.float32).at[dest].set(w_s)
    block_expert = jnp.minimum(
        jnp.searchsorted(pends, jnp.arange(n_blocks, dtype=jnp.int32) * MOE_BLOCK, side='right'),
        N_EXPERTS - 1)

    def run_block(args):
        toks, e = args
        h = x_tok[toks] @ w_up[e] + b_up[e]
        x_glu = jnp.minimum(h[:, :D_FF], SWIGLU_LIMIT)
        x_lin = jnp.clip(h[:, D_FF:], -SWIGLU_LIMIT, SWIGLU_LIMIT)
        act = x_glu * jax.nn.sigmoid(SWIGLU_ALPHA * x_glu) * (x_lin + 1.0)
        return act @ w_down[e] + b_down[e]

    y_blocks = lax.map(run_block, (tok_buf.reshape(n_blocks, MOE_BLOCK), block_expert))
    y_rows = y_blocks.reshape(n_blocks * MOE_BLOCK, -1) * w_buf[:, None].astype(y_blocks.dtype)
    return jnp.zeros_like(x_tok).at[tok_buf].add(y_rows.astype(x_tok.dtype))


def setup_inputs(seed: int = 0) -> dict:
    key = jax.random.key(seed)
    ks = jax.random.split(key, 32)
    n_pages = PAST_LEN // PAGE_SIZE
    n_used = DEC_BATCH * n_pages
    n_pool = n_used + n_used // 4

    def nrm(k, shape, scale=1.0):
        return jax.random.normal(k, shape, jnp.float32) * scale

    def gain(k, shape):
        return 1.0 + 0.02 * jax.random.normal(k, shape, jnp.float32)

    page_table = jax.random.permutation(ks[0], n_pool)[:n_used].reshape(DEC_BATCH, n_pages).astype(jnp.int32)
    return {
        'x_prompt': nrm(ks[1], (BATCH, SEQ, D_MODEL)),
        'x_sample': nrm(ks[2], (DEC_BATCH, DEC_SEQ, D_MODEL)),
        'cache_ckv': nrm(ks[3], (DEPTH, n_pool, PAGE_SIZE, KV_LORA)),
        'cache_krope': nrm(ks[4], (DEPTH, n_pool, PAGE_SIZE, MLA_ROPE_DIM)),
        'cache_sb_k': nrm(ks[5], (DEPTH, n_pool, PAGE_SIZE, SB_HEADS, SB_DIM)),
        'cache_sb_v': nrm(ks[6], (DEPTH, n_pool, PAGE_SIZE, SB_HEADS, SB_DIM)),
        'page_table': page_table,
        'g_attn_norm': gain(ks[7], (DEPTH, D_MODEL)),
        'w_in': nrm(ks[8], (DEPTH, D_MODEL, IN_WIDTH), D_MODEL ** -0.5),
        'g_q_norm': gain(ks[9], (DEPTH, Q_LORA)),
        'w_q_up': nrm(ks[10], (DEPTH, Q_LORA, MLA_HEADS, MLA_NOPE_DIM + MLA_ROPE_DIM), Q_LORA ** -0.5),
        'g_kv_norm': gain(ks[11], (DEPTH, KV_LORA)),
        'w_kv_up': nrm(ks[12], (DEPTH, KV_LORA, MLA_HEADS, MLA_NOPE_DIM + MLA_V_DIM), KV_LORA ** -0.5),
        'g_mla_out': gain(ks[13], (DEPTH, MLA_WIDTH)),
        'g_sb_out': gain(ks[14], (DEPTH, SB_WIDTH)),
        'w_out': nrm(ks[15], (DEPTH, MLA_WIDTH + SB_WIDTH, D_MODEL), (MLA_WIDTH + SB_WIDTH) ** -0.5),
        'g_moe_norm': gain(ks[16], (DEPTH, D_MODEL)),
        'w_router': nrm(ks[17], (DEPTH, D_MODEL, N_EXPERTS), D_MODEL ** -0.5),
        'b_router': nrm(ks[18], (DEPTH, N_EXPERTS), 0.01),
        'w_moe_up': nrm(ks[19], (DEPTH, N_EXPERTS, D_MODEL, 2 * D_FF), D_MODEL ** -0.5),
        'b_moe_up': nrm(ks[20], (DEPTH, N_EXPERTS, 2 * D_FF), 0.01),
        'w_moe_down': nrm(ks[21], (DEPTH, N_EXPERTS, D_FF, D_MODEL), D_FF ** -0.5),
        'b_moe_down': nrm(ks[22], (DEPTH, N_EXPERTS, D_MODEL), 0.01),
        'g_final': gain(ks[23], (D_MODEL,)),
    }


def reference(x_prompt, x_sample, cache_ckv, cache_krope, cache_sb_k, cache_sb_v, page_table,
              g_attn_norm, w_in, g_q_norm, w_q_up, g_kv_norm, w_kv_up, g_mla_out, g_sb_out, w_out,
              g_moe_norm, w_router, b_router, w_moe_up, b_moe_up, w_moe_down, b_moe_down, g_final):
    b_p, s_p, d = x_prompt.shape
    b_s, s_s, _ = x_sample.shape
    past_len = page_table.shape[1] * PAGE_SIZE
    pos_p = jnp.arange(s_p)
    pos_s = past_len + jnp.arange(s_s)
    h_p, h_s = x_prompt, x_sample
    ckv_ps, kr_ps, sbk_ps, sbv_ps = [], [], [], []
    ckv_ss, kr_ss, sbk_ss, sbv_ss = [], [], [], []
    for layer in range(DEPTH):
        w_uk = w_kv_up[layer, :, :, :MLA_NOPE_DIM]
        w_uv = w_kv_up[layer, :, :, MLA_NOPE_DIM:]
        qa, qr, c_p, kr_p, sq, sk_p, sv_p = mixer_inputs(
            rmsnorm(h_p, g_attn_norm[layer]), pos_p, w_in[layer], g_q_norm[layer], w_q_up[layer],
            g_kv_norm[layer], w_uk)
        lat_p, sb_p = prompt_attention(qa, qr, c_p, kr_p, sq, sk_p, sv_p, pos_p)
        h_p = h_p + mixer_output(lat_p, sb_p, w_uv, g_mla_out[layer], g_sb_out[layer], w_out[layer])
        qa, qr, c_s, kr_s, sq, sk_s, sv_s = mixer_inputs(
            rmsnorm(h_s, g_attn_norm[layer]), pos_s, w_in[layer], g_q_norm[layer], w_q_up[layer],
            g_kv_norm[layer], w_uk)
        lat_s, sb_s = sample_attention(qa, qr, c_s, kr_s, sq, sk_s, sv_s, cache_ckv, cache_krope,
                                       cache_sb_k, cache_sb_v, page_table, layer, past_len)
        h_s = h_s + mixer_output(lat_s, sb_s, w_uv, g_mla_out[layer], g_sb_out[layer], w_out[layer])
        tok = jnp.concatenate([rmsnorm(h_p, g_moe_norm[layer]).reshape(b_p * s_p, d),
                               rmsnorm(h_s, g_moe_norm[layer]).reshape(b_s * s_s, d)], axis=0)
        m = moe(tok, w_router[layer], b_router[layer], w_moe_up[layer], b_moe_up[layer],
                w_moe_down[layer], b_moe_down[layer])
        h_p = h_p + m[:b_p * s_p].reshape(b_p, s_p, d)
        h_s = h_s + m[b_p * s_p:].reshape(b_s, s_s, d)
        ckv_ps.append(c_p); kr_ps.append(kr_p); sbk_ps.append(sk_p); sbv_ps.append(sv_p)
        ckv_ss.append(c_s); kr_ss.append(kr_s); sbk_ss.append(sk_s); sbv_ss.append(sv_s)
    y_prompt = rmsnorm(h_p, g_final)
    y_sample = rmsnorm(h_s, g_final)
    return (y_prompt, y_sample,
            jnp.stack(ckv_ps), jnp.stack(kr_ps), jnp.stack(sbk_ps), jnp.stack(sbv_ps),
            jnp.stack(ckv_ss), jnp.stack(kr_ss), jnp.stack(sbk_ss), jnp.stack(sbv_ss))
```

```python
import functools
import math

import jax
import jax.numpy as jnp
from jax import lax
from jax.experimental import pallas as pl
from jax.experimental.pallas import tpu as pltpu

D_MODEL = 1024
HEAD_DIM = 64
N_HEADS = 8
NOPE = 64
ROPE = 32
HALF_ROPE = ROPE // 2
Q_LORA = 384
KV_LORA = 256
SB_WIDTH = N_HEADS * HEAD_DIM
MLA_WIDTH = N_HEADS * HEAD_DIM
HEAD_PAD = 128
QK_WIDTH = N_HEADS * HEAD_PAD
MLA_SCALE = 1.0 / math.sqrt(NOPE + ROPE)
SB_SCALE = 1.0 / math.sqrt(HEAD_DIM)
ROPE_THETA = 10000.0
N_EXPERTS = 32
TOP_K = 4
D_FF = 1024
SWIGLU_ALPHA = 1.702
SWIGLU_LIMIT = 7.0
PAGE = 128
EPS = 1e-6
LANES = 128
NEG_BIG = -1e30

_O_CQ = 0
_O_CKV = _O_CQ + Q_LORA
_O_SQ = _O_CKV + KV_LORA
_O_SK = _O_SQ + SB_WIDTH
_O_SV = _O_SK + SB_WIDTH
_O_KR = _O_SV + SB_WIDTH
IN_PAD = _O_KR + HEAD_PAD

VMEM_LIMIT = 56 * 1024 * 1024


def _cparams(sem, vmem=VMEM_LIMIT):
    return pltpu.CompilerParams(dimension_semantics=sem, vmem_limit_bytes=vmem)


def _rms(x, g):
    return x * lax.rsqrt(jnp.mean(x * x, axis=-1, keepdims=True) + EPS) * g


def _dot(a, b):
    return jnp.dot(a, b, preferred_element_type=jnp.float32)


def _dot_nt(a, b):
    return lax.dot_general(a, b, (((1,), (1,)), ((), ())), preferred_element_type=jnp.float32)


def _dot_tn(a, b):
    return lax.dot_general(a, b, (((0,), (0,)), ((), ())), preferred_element_type=jnp.float32)


def _rope_slot(t, cos_t, sin_t):
    lane = lax.broadcasted_iota(jnp.int32, t.shape, 1)
    partner = jnp.where(lane < NOPE + HALF_ROPE,
                        pltpu.roll(t, LANES - HALF_ROPE, 1),
                        pltpu.roll(t, HALF_ROPE, 1))
    return t * cos_t + partner * sin_t


def _proj_kernel(x_ref, cos_ref, sin_ref, g_attn_ref, w_in_ref, g_q_ref, w_q_ref, g_kv_ref,
                 w_ukp_ref, w_uvf_ref, *rest, sample):
    if sample:
        (w_ukt_ref, sel_ref, ckv_ref, kr_ref, sk_ref, sv_ref, sq_ref, qabs_ref, qrope_ref) = rest
    else:
        (ckv_ref, kr_ref, sk_ref, sv_ref, sq_ref, q_ref, k_ref, v_ref, skb_ref, svb_ref) = rest
    x = x_ref[...]
    xb = _rms(x, g_attn_ref[...]).astype(jnp.bfloat16)
    proj = _dot(xb, w_in_ref[...])
    cos_t = cos_ref[...]
    sin_t = sin_ref[...]

    cqn = _rms(proj[:, _O_CQ:_O_CKV], g_q_ref[...]).astype(jnp.bfloat16)
    q = _dot(cqn, w_q_ref[...])
    c_kv = _rms(proj[:, _O_CKV:_O_SQ], g_kv_ref[...])
    ckv_ref[...] = c_kv
    ckb = c_kv.astype(jnp.bfloat16)
    kr = _rope_slot(proj[:, _O_KR:IN_PAD], cos_t, sin_t)
    kr_ref[...] = kr[:, NOPE:NOPE + ROPE]
    sk = proj[:, _O_SK:_O_SV]
    sv = proj[:, _O_SV:_O_KR]
    sk_ref[...] = sk
    sv_ref[...] = sv
    sq_ref[...] = (proj[:, _O_SQ:_O_SK] * SB_SCALE).astype(jnp.bfloat16)

    q_heads = []
    for h in range(N_HEADS):
        qh = _rope_slot(q[:, h * HEAD_PAD:(h + 1) * HEAD_PAD], cos_t, sin_t) * MLA_SCALE
        q_heads.append(qh.astype(jnp.bfloat16))

    if sample:
        for h in range(N_HEADS):
            qabs_ref[:, h * KV_LORA:(h + 1) * KV_LORA] = _dot(q_heads[h], w_ukt_ref[h]).astype(jnp.bfloat16)
        qcat = jnp.concatenate(q_heads, axis=1)
        qrope_ref[...] = _dot(qcat, sel_ref[...]).astype(jnp.bfloat16)
    else:
        k_nope = _dot(ckb, w_ukp_ref[...])
        for h in range(N_HEADS):
            sl = slice(h * HEAD_PAD, (h + 1) * HEAD_PAD)
            q_ref[:, sl] = q_heads[h]
            k_ref[:, sl] = (k_nope[:, sl] + kr).astype(jnp.bfloat16)
        v_ref[...] = _dot(ckb, w_uvf_ref[...]).astype(jnp.bfloat16)
        skb_ref[...] = sk.astype(jnp.bfloat16)
        svb_ref[...] = sv.astype(jnp.bfloat16)


def _proj_call(x2d, cos_t, sin_t, wts, *, sample, seq_tiles, tm):
    n = x2d.shape[0]
    nt = n // tm
    tok = lambda w: pl.BlockSpec((tm, w), lambda i: (i, 0))
    tab = pl.BlockSpec((tm, LANES), lambda i: (i % seq_tiles, 0))
    full = lambda a: pl.BlockSpec(a.shape, lambda i: (0,) * a.ndim)
    f32, bf16 = jnp.float32, jnp.bfloat16
    sds = jax.ShapeDtypeStruct
    ins = [x2d, cos_t, sin_t, wts['g_attn'], wts['w_in'], wts['g_q'], wts['w_q'], wts['g_kv'],
           wts['w_ukp'], wts['w_uvf']]
    in_specs = [tok(D_MODEL), tab, tab] + [full(a) for a in ins[3:]]
    outs = [sds((n, KV_LORA), f32), sds((n, ROPE), f32), sds((n, SB_WIDTH), f32), sds((n, SB_WIDTH), f32),
            sds((n, SB_WIDTH), bf16)]
    out_specs = [tok(KV_LORA), tok(ROPE), tok(SB_WIDTH), tok(SB_WIDTH), tok(SB_WIDTH)]
    if sample:
        ins += [wts['w_ukt'], wts['sel']]
        in_specs += [full(wts['w_ukt']), full(wts['sel'])]
        outs += [sds((n, N_HEADS * KV_LORA), bf16), sds((n, N_HEADS * ROPE), bf16)]
        out_specs += [tok(N_HEADS * KV_LORA), tok(N_HEADS * ROPE)]
    else:
        outs += [sds((n, QK_WIDTH), bf16), sds((n, QK_WIDTH), bf16), sds((n, MLA_WIDTH), bf16),
                 sds((n, SB_WIDTH), bf16), sds((n, SB_WIDTH), bf16)]
        out_specs += [tok(QK_WIDTH), tok(QK_WIDTH), tok(MLA_WIDTH), tok(SB_WIDTH), tok(SB_WIDTH)]
    return pl.pallas_call(
        functools.partial(_proj_kernel, sample=sample),
        out_shape=outs, grid=(nt,), in_specs=in_specs, out_specs=out_specs,
        compiler_params=_cparams(("parallel",)),
        name="proj_sample" if sample else "proj_prompt",
    )(*ins)


def _rope_tables(pos):
    inv = ROPE_THETA ** (-jnp.arange(HALF_ROPE, dtype=jnp.float32) / HALF_ROPE)
    ang = pos.astype(jnp.float32)[:, None] * inv[None, :]
    cos, sin = jnp.cos(ang), jnp.sin(ang)
    t = pos.shape[0]
    ones, zeros = jnp.ones((t, NOPE), jnp.float32), jnp.zeros((t, NOPE), jnp.float32)
    pad = jnp.zeros((t, HEAD_PAD - NOPE - ROPE), jnp.float32)
    return (jnp.concatenate([ones, cos, cos, pad], axis=1),
            jnp.concatenate([zeros, -sin, sin, pad], axis=1))


def _prep_weights(g_attn_norm, w_in, g_q_norm, w_q_up, g_kv_norm, w_kv_up):
    bf16 = jnp.bfloat16
    o1, o2, o3 = Q_LORA, Q_LORA + KV_LORA, Q_LORA + KV_LORA + ROPE
    o4, o5 = o3 + SB_WIDTH, o3 + 2 * SB_WIDTH
    kr_slot = jnp.concatenate([jnp.zeros((D_MODEL, NOPE), w_in.dtype), w_in[:, o2:o3],
                               jnp.zeros((D_MODEL, HEAD_PAD - NOPE - ROPE), w_in.dtype)], axis=1)
    w_in_r = jnp.concatenate([w_in[:, :o2], w_in[:, o3:o4], w_in[:, o4:o5], w_in[:, o5:], kr_slot], axis=1)
    w_q_slot = jnp.concatenate([w_q_up, jnp.zeros((Q_LORA, N_HEADS, HEAD_PAD - NOPE - ROPE), w_q_up.dtype)],
                               axis=2).reshape(Q_LORA, QK_WIDTH)
    w_uk = w_kv_up[:, :, :NOPE]
    w_uv = w_kv_up[:, :, NOPE:]
    w_ukp = jnp.concatenate([w_uk, jnp.zeros_like(w_uk)], axis=2).reshape(KV_LORA, QK_WIDTH)
    w_ukt = jnp.concatenate([jnp.transpose(w_uk, (1, 2, 0)),
                             jnp.zeros((N_HEADS, HEAD_PAD - NOPE, KV_LORA), w_uk.dtype)], axis=1)
    r = jnp.arange(QK_WIDTH)
    c = jnp.arange(N_HEADS * ROPE)
    sel = ((r[:, None] // HEAD_PAD == c[None, :] // ROPE)
           & (r[:, None] % HEAD_PAD == NOPE + c[None, :] % ROPE)).astype(bf16)
    return dict(
        g_attn=g_attn_norm.reshape(1, D_MODEL), w_in=w_in_r.astype(bf16), g_q=g_q_norm.reshape(1, Q_LORA),
        w_q=w_q_slot.astype(bf16), g_kv=g_kv_norm.reshape(1, KV_LORA), w_ukp=w_ukp.astype(bf16),
        w_uvf=w_uv.reshape(KV_LORA, MLA_WIDTH).astype(bf16), w_ukt=w_ukt.astype(bf16), sel=sel)


def _mla_prompt_kernel(q_ref, k_ref, v_ref, o_ref, *, tq):
    qi = pl.program_id(1)
    row = lax.broadcasted_iota(jnp.int32, (tq, tq), 0)
    col = lax.broadcasted_iota(jnp.int32, (tq, tq), 1)
    causal = col <= row
    for h in range(N_HEADS):
        qh = q_ref[:, h * HEAD_PAD:(h + 1) * HEAD_PAD]

        def step(j, carry, masked, h=h, qh=qh):
            m, l, acc = carry
            ks = pl.multiple_of(j * tq, tq)
            k = k_ref[pl.ds(ks, tq), h * HEAD_PAD:(h + 1) * HEAD_PAD]
            v = v_ref[pl.ds(ks, tq), h * HEAD_DIM:(h + 1) * HEAD_DIM]
            s = _dot_nt(qh, k)
            if masked:
                s = jnp.where(causal, s, NEG_BIG)
            m_new = jnp.maximum(m, jnp.max(s, axis=1, keepdims=True))
            p = jnp.exp(s - m_new)
            alpha = jnp.exp(m - m_new)
            l = alpha * l + jnp.sum(p, axis=1, keepdims=True)
            acc = alpha * acc + _dot(p.astype(jnp.bfloat16), v)
            return m_new, l, acc

        init = (jnp.full((tq, 1), NEG_BIG, jnp.float32), jnp.zeros((tq, 1), jnp.float32),
                jnp.zeros((tq, HEAD_DIM), jnp.float32))
        carry = lax.fori_loop(0, qi, functools.partial(step, masked=False), init)
        _, l, acc = step(qi, carry, True)
        o_ref[:, h * HEAD_DIM:(h + 1) * HEAD_DIM] = acc / l


def _sb_prompt_kernel(q_ref, k_ref, v_ref, o_ref, *, tq):
    qi = pl.program_id(1)
    row = lax.broadcasted_iota(jnp.int32, (tq, tq), 0)
    col = lax.broadcasted_iota(jnp.int32, (tq, tq), 1)
    strict = col < row
    later = (row > col).astype(jnp.bfloat16)
    for h in range(N_HEADS):
        hs = slice(h * HEAD_DIM, (h + 1) * HEAD_DIM)
        qh = q_ref[:, hs]

        def step(j, carry, masked, hs=hs, qh=qh):
            r, acc = carry
            ks = pl.multiple_of(j * tq, tq)
            k = k_ref[pl.ds(ks, tq), hs]
            v = v_ref[pl.ds(ks, tq), hs]
            z = _dot_nt(qh, k)
            lg = -(jnp.maximum(z, 0.0) + jnp.log1p(jnp.exp(-jnp.abs(z))))
            if masked:
                lg = jnp.where(strict, lg, 0.0)
            hi = lg.astype(jnp.bfloat16)
            lo = (lg - hi.astype(jnp.float32)).astype(jnp.bfloat16)
            cum = _dot(hi, later) + _dot(lo, later)
            a = jnp.exp(z + lg + cum + r)
            if masked:
                a = jnp.where(strict, a, 0.0)
            acc = acc + _dot(a.astype(jnp.bfloat16), v)
            r = r + jnp.sum(lg, axis=1, keepdims=True)
            return r, acc

        carry = step(qi, (jnp.zeros((tq, 1), jnp.float32), jnp.zeros((tq, HEAD_DIM), jnp.float32)), True)
        _, acc = lax.fori_loop(0, qi, lambda i, c: step(qi - 1 - i, c, False), carry)
        o_ref[:, hs] = acc


def _prompt_attn_call(kern, q, k, v, *, batch, seq, tq, name):
    nq = seq // tq
    return pl.pallas_call(
        functools.partial(kern, tq=tq),
        out_shape=jax.ShapeDtypeStruct((batch * seq, N_HEADS * HEAD_DIM), jnp.float32),
        grid=(batch, nq),
        in_specs=[pl.BlockSpec((tq, q.shape[1]), lambda b, i: (b * nq + i, 0)),
                  pl.BlockSpec((seq, k.shape[1]), lambda b, i: (b, 0)),
                  pl.BlockSpec((seq, v.shape[1]), lambda b, i: (b, 0))],
        out_specs=pl.BlockSpec((tq, N_HEADS * HEAD_DIM), lambda b, i: (b * nq + i, 0)),
        compiler_params=_cparams(("parallel", "arbitrary")),
        name=name,
    )(q, k, v)


N_QH = 64


def _softplus(z):
    return jnp.maximum(z, 0.0) + jnp.log1p(jnp.exp(-jnp.abs(z)))


def _split_bf16(x):
    hi = x.astype(jnp.bfloat16)
    return hi, (x - hi.astype(jnp.float32)).astype(jnp.bfloat16)


def _own_head(full):
    rowi = lax.broadcasted_iota(jnp.int32, full.shape, 0)
    lanei = lax.broadcasted_iota(jnp.int32, full.shape, 1)
    kept = jnp.where(lanei // HEAD_DIM == rowi % N_HEADS, full, 0.0)
    return jnp.sum(kept.reshape(N_QH // N_HEADS, N_HEADS, full.shape[1]), axis=1)


def _pad_rows(x, rows):
    return jnp.concatenate([x, jnp.zeros((rows - x.shape[0], x.shape[1]), x.dtype)], axis=0)


def _sample_attn_kernel(pt_ref, qabs_ref, qrope_ref, sq_ref, cn_ref, krn_ref, skn_ref, svn_ref, wuv_ref, *rest,
                        n_pg):
    del pt_ref
    ckv_pg = rest[0:n_pg]
    kr_pg = rest[n_pg:2 * n_pg]
    sbk_pg = rest[2 * n_pg:3 * n_pg]
    sbv_pg = rest[3 * n_pg:4 * n_pg]
    o_mla_ref, o_sb_ref, m_sc, l_sc, acc_sc, r_sc, accsb_sc, qbd_sc = rest[4 * n_pg:]
    c = pl.program_id(1)
    bf16 = jnp.bfloat16
    qabs = qabs_ref[...]
    qrope = qrope_ref[...]

    later = (lax.broadcasted_iota(jnp.int32, (PAGE, PAGE), 0)
             > lax.broadcasted_iota(jnp.int32, (PAGE, PAGE), 1)).astype(bf16)

    def suffix_in_block(lg):
        hi, lo = _split_bf16(lg)
        both = _dot(jnp.concatenate([hi, lo], axis=0), later)
        return both[:N_QH] + both[N_QH:]

    @pl.when(c == 0)
    def _new_rows():
        sq16 = _pad_rows(sq_ref[...], 16).astype(bf16)
        pick = (lax.broadcasted_iota(jnp.int32, (N_QH, 16), 0) // N_HEADS
                == lax.broadcasted_iota(jnp.int32, (N_QH, 16), 1)).astype(bf16)
        rep = _dot(pick, sq16)
        rowi = lax.broadcasted_iota(jnp.int32, rep.shape, 0)
        lanei = lax.broadcasted_iota(jnp.int32, rep.shape, 1)
        qbd = jnp.where(lanei // HEAD_DIM == rowi % N_HEADS, rep, 0.0).astype(bf16)
        qbd_sc[...] = qbd

        key = lax.broadcasted_iota(jnp.int32, (N_QH, PAGE), 1)
        qry = lax.broadcasted_iota(jnp.int32, (N_QH, PAGE), 0) // N_HEADS
        cn = _pad_rows(cn_ref[...], PAGE).astype(bf16)
        krn = _pad_rows(krn_ref[...], PAGE).astype(bf16)
        s = _dot_nt(qabs, cn) + _dot_nt(qrope, krn)
        s = jnp.where(key <= qry, s, NEG_BIG)
        m = jnp.max(s, axis=1, keepdims=True)
        p = jnp.exp(s - m)
        m_sc[...] = m
        l_sc[...] = jnp.sum(p, axis=1, keepdims=True)
        acc_sc[...] = _dot(p.astype(bf16), cn)

        strict = key < qry
        z = _dot_nt(qbd, _pad_rows(skn_ref[...], PAGE).astype(bf16))
        lg = jnp.where(strict, -_softplus(z), 0.0)
        a = jnp.where(strict, jnp.exp(z + lg + suffix_in_block(lg)), 0.0)
        accsb_sc[...] = _dot(a.astype(bf16), _pad_rows(svn_ref[...], PAGE).astype(bf16))
        r_sc[...] = jnp.sum(lg, axis=1, keepdims=True)

    qbd = qbd_sc[...]

    ck = [ckv_pg[p][...].astype(bf16) for p in range(n_pg)]
    s_pg = [_dot_nt(qabs, ck[p]) + _dot(qrope, kr_pg[p][...].astype(bf16)) for p in range(n_pg)]
    m_old = m_sc[...]
    m_new = m_old
    for p in range(n_pg):
        m_new = jnp.maximum(m_new, jnp.max(s_pg[p], axis=1, keepdims=True))
    alpha = jnp.exp(m_old - m_new)
    l_new = alpha * l_sc[...]
    pv = jnp.zeros((N_QH, KV_LORA), jnp.float32)
    for p in range(n_pg):
        e = jnp.exp(s_pg[p] - m_new)
        l_new = l_new + jnp.sum(e, axis=1, keepdims=True)
        pv = pv + _dot(e.astype(bf16), ck[p])
    m_sc[...] = m_new
    l_sc[...] = l_new
    acc_sc[...] = acc_sc[...] * alpha + pv

    r = r_sc[...]
    sb = jnp.zeros((N_QH, SB_WIDTH), jnp.float32)
    for p in range(n_pg):
        z = _dot(qbd, sbk_pg[p][...].astype(bf16))
        lg = -_softplus(z)
        a = jnp.exp(z + lg + suffix_in_block(lg) + r)
        sb = sb + _dot_nt(a.astype(bf16), sbv_pg[p][...].astype(bf16))
        r = r + jnp.sum(lg, axis=1, keepdims=True)
    r_sc[...] = r
    accsb_sc[...] = accsb_sc[...] + sb

    @pl.when(c == pl.num_programs(1) - 1)
    def _finish():
        o_lat = acc_sc[...] / l_sc[...]
        o_mla_ref[...] = _own_head(_dot(o_lat.astype(bf16), wuv_ref[...]))
        o_sb_ref[...] = _own_head(accsb_sc[...])


def _feature_major(cache):
    return jnp.moveaxis(cache, 1, -1).reshape(cache.shape[0], -1, PAGE)


def _sample_attn_call(page_table, qabs, qrope, sq, ckv_n, kr_n, sk_n, sv_n, w_uvf, cache_ckv, cache_krope,
                      cache_sb_k, cache_sb_v, *, n_pg):
    n_seq, n_pages = page_table.shape
    n_chunks = n_pages // n_pg
    t_new = ckv_n.shape[0] // n_seq

    def seq_blk(shape):
        return pl.BlockSpec((None,) + shape, lambda s, c, pt: (s, 0, 0))

    def new_blk(w):
        return pl.BlockSpec((t_new, w), lambda s, c, pt: (s, 0))

    def page_blk(shape, p):
        return pl.BlockSpec((None,) + shape, lambda s, c, pt: (pt[s, n_pages - 1 - (c * n_pg + p)], 0, 0))

    in_specs = [seq_blk((N_QH, KV_LORA)), seq_blk((N_QH, ROPE)), new_blk(SB_WIDTH), new_blk(KV_LORA),
                new_blk(ROPE), new_blk(SB_WIDTH), new_blk(SB_WIDTH),
                pl.BlockSpec(w_uvf.shape, lambda s, c, pt: (0, 0))]
    args = [qabs.reshape(n_seq, N_QH, KV_LORA), qrope.reshape(n_seq, N_QH, ROPE), sq, ckv_n, kr_n, sk_n, sv_n, w_uvf]
    for arr, shape in ((cache_ckv, (PAGE, KV_LORA)), (cache_krope, (ROPE, PAGE)), (cache_sb_k, (SB_WIDTH, PAGE)),
                       (cache_sb_v, (SB_WIDTH, PAGE))):
        for p in range(n_pg):
            in_specs.append(page_blk(shape, p))
            args.append(arr)
    out_blk = pl.BlockSpec((t_new, MLA_WIDTH), lambda s, c, pt: (s, 0))
    f32 = jnp.float32
    return pl.pallas_call(
        functools.partial(_sample_attn_kernel, n_pg=n_pg),
        out_shape=[jax.ShapeDtypeStruct((n_seq * t_new, MLA_WIDTH), f32),
                   jax.ShapeDtypeStruct((n_seq * t_new, SB_WIDTH), f32)],
        grid_spec=pltpu.PrefetchScalarGridSpec(
            num_scalar_prefetch=1, grid=(n_seq, n_chunks), in_specs=in_specs, out_specs=[out_blk, out_blk],
            scratch_shapes=[pltpu.VMEM((N_QH, 1), f32), pltpu.VMEM((N_QH, 1), f32), pltpu.VMEM((N_QH, KV_LORA), f32),
                            pltpu.VMEM((N_QH, 1), f32), pltpu.VMEM((N_QH, SB_WIDTH), f32),
                            pltpu.VMEM((N_QH, SB_WIDTH), jnp.bfloat16)]),
        compiler_params=_cparams(("parallel", "arbitrary")),
        name="sample_attn",
    )(page_table, *args)


def _lane_pack(cols):
    lane = lax.broadcasted_iota(jnp.int32, (cols[0].shape[0], LANES), 1)
    out = jnp.zeros((cols[0].shape[0], LANES), jnp.float32)
    for k, col in enumerate(cols):
        out = jnp.where(lane == k, col, out)
    return out


def _out_router_kernel(omla_ref, osb_ref, x_ref, gm_ref, gs_ref, wout_ref, gmoe_ref, wr_hi_ref, wr_lo_ref, br_ref,
                       h_ref, tok_ref, eidx_ref, gate_ref, rank_ref, cnt_ref, carry_sc, *, tm):
    bf16 = jnp.bfloat16

    @pl.when(pl.program_id(0) == 0)
    def _():
        carry_sc[...] = jnp.zeros_like(carry_sc)

    o = jnp.concatenate([_rms(omla_ref[...], gm_ref[...]), _rms(osb_ref[...], gs_ref[...])], axis=1).astype(bf16)
    h = x_ref[...] + _dot(o, wout_ref[...])
    h_ref[...] = h
    tok = _rms(h, gmoe_ref[...])
    tok_ref[...] = tok
    t_hi, t_lo = _split_bf16(tok)
    wr_hi = wr_hi_ref[...]
    logits = _dot(t_hi, wr_hi) + (_dot(t_hi, wr_lo_ref[...]) + _dot(t_lo, wr_hi)) + br_ref[...]

    lane = lax.broadcasted_iota(jnp.int32, (tm, LANES), 1).astype(jnp.float32)
    work = logits
    vals, idxs, sels = [], [], []
    for _ in range(TOP_K):
        v = jnp.max(work, axis=1, keepdims=True)
        idx = jnp.min(jnp.where(work == v, lane, float(LANES)), axis=1, keepdims=True)
        sel = lane == idx
        work = jnp.where(sel, -jnp.inf, work)
        vals.append(v)
        idxs.append(idx)
        sels.append(sel)
    exps = [jnp.exp(v - vals[0]) for v in vals]
    denom = exps[0] + exps[1] + exps[2] + exps[3]
    gates = [e / denom for e in exps]

    assign = jnp.zeros((tm, LANES), jnp.float32)
    for sel in sels:
        assign = jnp.where(sel, 1.0, assign)
    earlier = (lax.broadcasted_iota(jnp.int32, (tm, tm), 1)
               < lax.broadcasted_iota(jnp.int32, (tm, tm), 0)).astype(bf16)
    prefix = _dot(earlier, assign.astype(bf16)) + carry_sc[...]
    ranks = [jnp.sum(jnp.where(sel, prefix, 0.0), axis=1, keepdims=True) for sel in sels]
    carry = carry_sc[...] + jnp.sum(assign, axis=0, keepdims=True)
    carry_sc[...] = carry
    cnt_ref[...] = carry
    eidx_ref[...] = _lane_pack(idxs)
    gate_ref[...] = _lane_pack(gates)
    rank_ref[...] = _lane_pack(ranks)


def _out_router_call(o_mla, o_sb, x2d, wts, *, tm):
    n = x2d.shape[0]
    tok = lambda w: pl.BlockSpec((tm, w), lambda i: (i, 0))
    full = lambda a: pl.BlockSpec(a.shape, lambda i: (0,) * a.ndim)
    f32 = jnp.float32
    sds = jax.ShapeDtypeStruct
    ws = [wts['g_mla_out'], wts['g_sb_out'], wts['w_out'], wts['g_moe'], wts['wr_hi'], wts['wr_lo'], wts['b_router']]
    return pl.pallas_call(
        functools.partial(_out_router_kernel, tm=tm),
        out_shape=[sds((n, D_MODEL), f32), sds((n, D_MODEL), f32), sds((n, LANES), f32), sds((n, LANES), f32),
                   sds((n, LANES), f32), sds((1, LANES), f32)],
        grid=(n // tm,),
        in_specs=[tok(MLA_WIDTH), tok(SB_WIDTH), tok(D_MODEL)] + [full(a) for a in ws],
        out_specs=[tok(D_MODEL), tok(D_MODEL), tok(LANES), tok(LANES), tok(LANES),
                   pl.BlockSpec((1, LANES), lambda i: (0, 0))],
        scratch_shapes=[pltpu.VMEM((1, LANES), f32)],
        compiler_params=_cparams(("arbitrary",)),
        name="out_router",
    )(o_mla, o_sb, x2d, *ws)


def _row_copy(src, s, dst, d, sem):
    return pltpu.make_async_copy(src.at[pl.ds(s, 1), :], dst.at[pl.ds(d, 1), :], sem)


def _dispatch_kernel(dest_ref, tokp_ref, toks_ref, xs_ref, sem, *, tm, n_prompt_tiles):
    i = pl.program_id(0)

    def scatter(tok_ref):
        def start(t, _):
            for k in range(TOP_K):
                _row_copy(tok_ref, t, xs_ref, dest_ref[0, t * TOP_K + k], sem).start()
            return 0

        def wait(t, _):
            for k in range(TOP_K):
                _row_copy(tok_ref, 0, xs_ref, 0, sem).wait()
            return 0

        lax.fori_loop(0, tm, start, 0)
        lax.fori_loop(0, tm, wait, 0)

    @pl.when(i < n_prompt_tiles)
    def _():
        scatter(tokp_ref)

    @pl.when(i >= n_prompt_tiles)
    def _():
        scatter(toks_ref)


def _dispatch_call(dest_tiles, tok_p, tok_s, *, tm):
    npt, nst = tok_p.shape[0] // tm, tok_s.shape[0] // tm
    n_pairs = (tok_p.shape[0] + tok_s.shape[0]) * TOP_K
    return pl.pallas_call(
        functools.partial(_dispatch_kernel, tm=tm, n_prompt_tiles=npt),
        out_shape=jax.ShapeDtypeStruct((n_pairs, D_MODEL), jnp.float32),
        grid=(npt + nst,),
        in_specs=[pl.BlockSpec((None, 1, tm * TOP_K), lambda i: (i, 0, 0), memory_space=pltpu.SMEM),
                  pl.BlockSpec((tm, D_MODEL), lambda i: (jnp.minimum(i, npt - 1), 0)),
                  pl.BlockSpec((tm, D_MODEL), lambda i: (jnp.maximum(i - npt, 0), 0))],
        out_specs=pl.BlockSpec(memory_space=pl.ANY),
        scratch_shapes=[pltpu.SemaphoreType.DMA],
        compiler_params=_cparams(("arbitrary",)),
        name="moe_dispatch",
    )(dest_tiles, tok_p, tok_s)


def _expert_kernel(blk_ref, exp_ref, lo_ref, hi_ref, first_ref, xs_ref, wup_ref, bup_ref, wdn_ref, bdn_ref,
                   y_ref, wup_sc, wdn_sc, *, bm):
    i = pl.program_id(0)
    bf16 = jnp.bfloat16
    lo, hi = lo_ref[i], hi_ref[i]
    e_now = exp_ref[i]
    e_prev = exp_ref[jnp.maximum(i - 1, 0)]

    @pl.when((i == 0) | (e_now != e_prev))
    def _():
        wup_sc[...] = wup_ref[...].astype(bf16)
        wdn_sc[...] = wdn_ref[...].astype(bf16)

    @pl.when(hi > lo)
    def _():
        x = xs_ref[...].astype(bf16)
        hcat = _dot(x, wup_sc[...]) + bup_ref[...]
        x_glu = jnp.minimum(hcat[:, :D_FF], SWIGLU_LIMIT)
        x_lin = jnp.clip(hcat[:, D_FF:], -SWIGLU_LIMIT, SWIGLU_LIMIT)
        act = x_glu * jax.nn.sigmoid(SWIGLU_ALPHA * x_glu) * (x_lin + 1.0)
        y = _dot(act.astype(bf16), wdn_sc[...]) + bdn_ref[...]
        rows = lax.broadcasted_iota(jnp.int32, (bm, 1), 0)
        y = jnp.where((rows >= lo) & (rows < hi), y, 0.0)

        @pl.when(first_ref[i] == 1)
        def _():
            y_ref[...] = y

        @pl.when(first_ref[i] == 0)
        def _():
            y_ref[...] = y_ref[...] + y


def _expert_call(items, xs, w_up, b_up, w_dn, b_dn, *, bm):
    n_items = items[0].shape[0]
    return pl.pallas_call(
        functools.partial(_expert_kernel, bm=bm),
        out_shape=jax.ShapeDtypeStruct(xs.shape, jnp.float32),
        grid_spec=pltpu.PrefetchScalarGridSpec(
            num_scalar_prefetch=5, grid=(n_items,),
            in_specs=[pl.BlockSpec((bm, D_MODEL), lambda i, blk, ex, lo, hi, fi: (blk[i], 0)),
                      pl.BlockSpec((None, D_MODEL, 2 * D_FF), lambda i, blk, ex, lo, hi, fi: (ex[i], 0, 0)),
                      pl.BlockSpec((None, 1, 2 * D_FF), lambda i, blk, ex, lo, hi, fi: (ex[i], 0, 0)),
                      pl.BlockSpec((None, D_FF, D_MODEL), lambda i, blk, ex, lo, hi, fi: (ex[i], 0, 0)),
                      pl.BlockSpec((None, 1, D_MODEL), lambda i, blk, ex, lo, hi, fi: (ex[i], 0, 0))],
            out_specs=pl.BlockSpec((bm, D_MODEL), lambda i, blk, ex, lo, hi, fi: (blk[i], 0)),
            scratch_shapes=[pltpu.VMEM((D_MODEL, 2 * D_FF), jnp.bfloat16), pltpu.VMEM((D_FF, D_MODEL), jnp.bfloat16)]),
        compiler_params=_cparams(("arbitrary",)),
        name="moe_experts",
    )(*items, xs, w_up, b_up.reshape(N_EXPERTS, 1, 2 * D_FF), w_dn, b_dn.reshape(N_EXPERTS, 1, D_MODEL))


def _combine_kernel(dest_ref, h_ref, gate_ref, gf_ref, ys_ref, out_ref, buf, sem, *, tm):
    def start(t, _):
        for k in range(TOP_K):
            pltpu.make_async_copy(ys_ref.at[pl.ds(dest_ref[0, t * TOP_K + k], 1), :],
                                  buf.at[k, pl.ds(t, 1), :], sem).start()
        return 0

    def wait(t, _):
        for k in range(TOP_K):
            pltpu.make_async_copy(ys_ref.at[pl.ds(0, 1), :], buf.at[0, pl.ds(0, 1), :], sem).wait()
        return 0

    lax.fori_loop(0, tm, start, 0)
    lax.fori_loop(0, tm, wait, 0)
    gate = gate_ref[...]
    h = h_ref[...]
    for k in range(TOP_K):
        h = h + gate[:, k:k + 1] * buf[k]
    out_ref[...] = _rms(h, gf_ref[...])


def _combine_call(dest_tiles, h, gate, g_final, ys, *, tm):
    n = h.shape[0]
    return pl.pallas_call(
        functools.partial(_combine_kernel, tm=tm),
        out_shape=jax.ShapeDtypeStruct((n, D_MODEL), jnp.float32),
        grid=(n // tm,),
        in_specs=[pl.BlockSpec((None, 1, tm * TOP_K), lambda i: (i, 0, 0), memory_space=pltpu.SMEM),
                  pl.BlockSpec((tm, D_MODEL), lambda i: (i, 0)),
                  pl.BlockSpec((tm, LANES), lambda i: (i, 0)),
                  pl.BlockSpec((1, D_MODEL), lambda i: (0, 0)),
                  pl.BlockSpec(memory_space=pl.ANY)],
        out_specs=pl.BlockSpec((tm, D_MODEL), lambda i: (i, 0)),
        scratch_shapes=[pltpu.VMEM((TOP_K, tm, D_MODEL), jnp.float32), pltpu.SemaphoreType.DMA],
        compiler_params=_cparams(("arbitrary",)),
        name="moe_combine",
    )(dest_tiles, h, gate, g_final.reshape(1, D_MODEL), ys)


def _plan_items(counts, n_pairs, bm):
    n_blocks = n_pairs // bm
    n_items = n_blocks + N_EXPERTS - 1
    ends = jnp.cumsum(counts)
    starts = ends - counts
    blk_lo = jnp.arange(n_blocks, dtype=jnp.int32) * bm
    first_blk = starts // bm
    last_blk = jnp.where(counts > 0, (ends - 1) // bm, first_blk - 1)
    n_touch = jnp.maximum(last_blk - first_blk + 1, 0)
    item_end = jnp.cumsum(n_touch)
    item_start = item_end - n_touch
    total = item_end[-1]
    it = jnp.arange(n_items, dtype=jnp.int32)
    e_of = jnp.minimum(jnp.searchsorted(item_end, it, side='right'), N_EXPERTS - 1).astype(jnp.int32)
    blk_of = first_blk[e_of] + (it - item_start[e_of])
    valid = it < total
    last_e = e_of[jnp.maximum(total - 1, 0)]
    last_b = blk_of[jnp.maximum(total - 1, 0)]
    e_of = jnp.where(valid, e_of, last_e)
    blk_of = jnp.where(valid, blk_of, last_b).astype(jnp.int32)
    lo = jnp.clip(starts[e_of] - blk_of * bm, 0, bm)
    hi = jnp.clip(ends[e_of] - blk_of * bm, 0, bm)
    lo = jnp.where(valid, lo, 0).astype(jnp.int32)
    hi = jnp.where(valid, hi, 0).astype(jnp.int32)
    prev_blk = jnp.concatenate([jnp.full((1,), -1, jnp.int32), blk_of[:-1]])
    first = (valid & (blk_of != prev_blk)).astype(jnp.int32)
    del blk_lo
    return blk_of, e_of.astype(jnp.int32), lo, hi, first


def kernel(x_prompt, x_sample, cache_ckv, cache_krope, cache_sb_k, cache_sb_v, page_table, g_attn_norm, w_in,
           g_q_norm, w_q_up, g_kv_norm, w_kv_up, g_mla_out, g_sb_out, w_out, g_moe_norm, w_router, b_router,
           w_moe_up, b_moe_up, w_moe_down, b_moe_down, g_final):
    b_p, s_p, d = x_prompt.shape
    b_s, s_s, _ = x_sample.shape
    n_p, n_s = b_p * s_p, b_s * s_s
    n_pool = cache_ckv.shape[1]
    past_len = page_table.shape[1] * PAGE
    bf16 = jnp.bfloat16
    wts = _prep_weights(g_attn_norm[0], w_in[0], g_q_norm[0], w_q_up[0], g_kv_norm[0], w_kv_up[0])
    w_r = jnp.pad(w_router[0], ((0, 0), (0, LANES - N_EXPERTS)))
    wr_hi = w_r.astype(bf16)
    wts.update(
        g_mla_out=g_mla_out[0].reshape(1, MLA_WIDTH), g_sb_out=g_sb_out[0].reshape(1, SB_WIDTH),
        w_out=w_out[0].astype(bf16), g_moe=g_moe_norm[0].reshape(1, D_MODEL), wr_hi=wr_hi,
        wr_lo=(w_r - wr_hi.astype(jnp.float32)).astype(bf16),
        b_router=jnp.pad(b_router[0], (0, LANES - N_EXPERTS), constant_values=NEG_BIG).reshape(1, LANES))
    x_p2 = x_prompt.reshape(n_p, d)
    x_s2 = x_sample.reshape(n_s, d)
    tm = 256

    cos_p, sin_p = _rope_tables(jnp.arange(s_p))
    (ckv_p, kr_p, sk_p, sv_p, sq_p, q_p, k_p, v_p, skb_p, svb_p) = _proj_call(
        x_p2, cos_p, sin_p, wts, sample=False, seq_tiles=s_p // tm, tm=tm)
    o_mla_p = _prompt_attn_call(_mla_prompt_kernel, q_p, k_p, v_p, batch=b_p, seq=s_p, tq=256, name="mla_prompt")
    o_sb_p = _prompt_attn_call(_sb_prompt_kernel, sq_p, skb_p, svb_p, batch=b_p, seq=s_p, tq=256, name="sb_prompt")
    h_p, tok_p, eidx_p, gate_p, rank_p, cnt_p = _out_router_call(o_mla_p, o_sb_p, x_p2, wts, tm=tm)

    cos_s, sin_s = _rope_tables(past_len + jnp.arange(s_s))
    cos_s, sin_s = jnp.tile(cos_s, (b_s, 1)), jnp.tile(sin_s, (b_s, 1))
    (ckv_s, kr_s, sk_s, sv_s, sq_s, qabs_s, qrope_s) = _proj_call(
        x_s2, cos_s, sin_s, wts, sample=True, seq_tiles=n_s // tm, tm=tm)
    o_mla_s, o_sb_s = _sample_attn_call(
        page_table, qabs_s, qrope_s, sq_s.astype(jnp.float32), ckv_s, kr_s, sk_s, sv_s, wts['w_uvf'],
        cache_ckv[0], _feature_major(cache_krope[0]), _feature_major(cache_sb_k[0]), _feature_major(cache_sb_v[0]),
        n_pg=8)
    h_s, tok_s, eidx_s, gate_s, rank_s, cnt_s = _out_router_call(o_mla_s, o_sb_s, x_s2, wts, tm=tm)

    cnt_p = cnt_p[0, :N_EXPERTS].astype(jnp.int32)
    cnt_s = cnt_s[0, :N_EXPERTS].astype(jnp.int32)
    counts = cnt_p + cnt_s
    pstart = jnp.cumsum(counts) - counts
    e_p = eidx_p[:, :TOP_K].astype(jnp.int32)
    e_s = eidx_s[:, :TOP_K].astype(jnp.int32)
    dest_p = pstart[e_p] + rank_p[:, :TOP_K].astype(jnp.int32)
    dest_s = pstart[e_s] + cnt_p[e_s] + rank_s[:, :TOP_K].astype(jnp.int32)
    dest_p = dest_p.reshape(n_p // tm, 1, tm * TOP_K)
    dest_s = dest_s.reshape(n_s // tm, 1, tm * TOP_K)
    bm = 256
    items = _plan_items(counts, (n_p + n_s) * TOP_K, bm)

    xs = _dispatch_call(jnp.concatenate([dest_p, dest_s], axis=0), tok_p, tok_s, tm=tm)
    ys = _expert_call(items, xs, w_moe_up[0], b_moe_up[0], w_moe_down[0], b_moe_down[0], bm=bm)
    y_p = _combine_call(dest_p, h_p, gate_p, g_final, ys, tm=tm)
    y_s = _combine_call(dest_s, h_s, gate_s, g_final, ys, tm=tm)

    return (y_p.reshape(b_p, s_p, d), y_s.reshape(b_s, s_s, d),
            ckv_p.reshape(1, b_p, s_p, KV_LORA), kr_p.reshape(1, b_p, s_p, ROPE),
            sk_p.reshape(1, b_p, s_p, N_HEADS, HEAD_DIM), sv_p.reshape(1, b_p, s_p, N_HEADS, HEAD_DIM),
            ckv_s.reshape(1, b_s, s_s, KV_LORA), kr_s.reshape(1, b_s, s_s, ROPE),
            sk_s.reshape(1, b_s, s_s, N_HEADS, HEAD_DIM), sv_s.reshape(1, b_s, s_s, N_HEADS, HEAD_DIM))
```

```python
import functools
import math

import jax
import jax.numpy as jnp
from jax import lax
from jax.experimental import pallas as pl
from jax.experimental.pallas import tpu as pltpu

D_MODEL = 1024
HEAD_DIM = 64
N_HEADS = 8
NOPE = 64
ROPE = 32
HALF_ROPE = ROPE // 2
Q_LORA = 384
KV_LORA = 256
SB_WIDTH = N_HEADS * HEAD_DIM
MLA_WIDTH = N_HEADS * HEAD_DIM
HEAD_PAD = 128
QK_WIDTH = N_HEADS * HEAD_PAD
MLA_SCALE = 1.0 / math.sqrt(NOPE + ROPE)
SB_SCALE = 1.0 / math.sqrt(HEAD_DIM)
ROPE_THETA = 10000.0
N_EXPERTS = 32
TOP_K = 4
D_FF = 1024
SWIGLU_ALPHA = 1.702
SWIGLU_LIMIT = 7.0
PAGE = 128
EPS = 1e-6
LANES = 128
NEG_BIG = -1e30
LOG2E = 1.4426950408889634
SB_DEAD = 104.0

_O_CQ = 0
_O_CKV = _O_CQ + Q_LORA
_O_SQ = _O_CKV + KV_LORA
_O_SK = _O_SQ + SB_WIDTH
_O_SV = _O_SK + SB_WIDTH
_O_KR = _O_SV + SB_WIDTH
IN_PAD = _O_KR + HEAD_PAD

VMEM_LIMIT = 56 * 1024 * 1024


def _cparams(sem, vmem=VMEM_LIMIT):
    return pltpu.CompilerParams(dimension_semantics=sem, vmem_limit_bytes=vmem)


def _rms(x, g):
    return x * lax.rsqrt(jnp.mean(x * x, axis=-1, keepdims=True) + EPS) * g


def _dot(a, b):
    return jnp.dot(a, b, preferred_element_type=jnp.float32)


def _dot_nt(a, b):
    return lax.dot_general(a, b, (((1,), (1,)), ((), ())), preferred_element_type=jnp.float32)


def _dot_tn(a, b):
    return lax.dot_general(a, b, (((0,), (0,)), ((), ())), preferred_element_type=jnp.float32)


def _rope_slot(t, cos_t, sin_t):
    lane = lax.broadcasted_iota(jnp.int32, t.shape, 1)
    partner = jnp.where(lane < NOPE + HALF_ROPE,
                        pltpu.roll(t, LANES - HALF_ROPE, 1),
                        pltpu.roll(t, HALF_ROPE, 1))
    return t * cos_t + partner * sin_t


def _proj_kernel(x_ref, cos_ref, sin_ref, g_attn_ref, w_in_ref, g_q_ref, w_q_ref, g_kv_ref,
                 w_ukp_ref, w_uvt_ref, *rest, sample):
    bf16 = jnp.bfloat16
    if sample:
        (w_ukt_ref, sel_ref, ckv_ref, kr_ref, sk_ref, sv_ref, sq_ref, qabs_ref, qrope_ref) = rest
    else:
        (ckv_ref, krt_ref, skt_ref, svt_ref, qt_ref, k_ref, vt_ref, sqt_ref, skb_ref, svtb_ref) = rest
    x = x_ref[...]
    xb = _rms(x, g_attn_ref[...]).astype(bf16)
    proj = _dot(xb, w_in_ref[...])
    cos_t = cos_ref[...]
    sin_t = sin_ref[...]

    cqn = _rms(proj[:, _O_CQ:_O_CKV], g_q_ref[...]).astype(bf16)
    q = _dot(cqn, w_q_ref[...])
    c_kv = _rms(proj[:, _O_CKV:_O_SQ], g_kv_ref[...])
    ckv_ref[...] = c_kv
    ckb = c_kv.astype(bf16)
    kr = _rope_slot(proj[:, _O_KR:IN_PAD], cos_t, sin_t)
    sq = proj[:, _O_SQ:_O_SK] * SB_SCALE
    sk = proj[:, _O_SK:_O_SV]
    sv = proj[:, _O_SV:_O_KR]
    q_heads = [_rope_slot(q[:, h * HEAD_PAD:(h + 1) * HEAD_PAD], cos_t, sin_t) * (MLA_SCALE * LOG2E)
               for h in range(N_HEADS)]

    if sample:
        kr_ref[...] = kr[:, NOPE:NOPE + ROPE]
        sk_ref[...] = sk
        sv_ref[...] = sv
        sq_ref[...] = sq.astype(bf16)
        for h in range(N_HEADS):
            qabs_ref[:, h * KV_LORA:(h + 1) * KV_LORA] = _dot(q_heads[h].astype(bf16), w_ukt_ref[h]).astype(bf16)
        qcat = jnp.concatenate([qh.astype(bf16) for qh in q_heads], axis=1)
        qrope_ref[...] = _dot(qcat, sel_ref[...]).astype(bf16)
    else:
        krt_ref[...] = kr.T[NOPE:NOPE + ROPE, :]
        sk_t = sk.T
        sv_t = sv.T
        skt_ref[...] = sk_t
        svt_ref[...] = sv_t
        svtb_ref[...] = sv_t.astype(bf16)
        skb_ref[...] = sk.astype(bf16)
        sqt_ref[...] = sq.T.astype(bf16)
        k_nope = _dot(ckb, w_ukp_ref[...])
        for h in range(N_HEADS):
            sl = slice(h * HEAD_PAD, (h + 1) * HEAD_PAD)
            qt_ref[sl, :] = q_heads[h].T.astype(bf16)
            k_ref[:, sl] = (k_nope[:, sl] + kr).astype(bf16)
        vt_ref[...] = _dot_nt(w_uvt_ref[...], ckb).astype(bf16)


def _proj_call(x2d, cos_t, sin_t, wts, *, sample, seq_tiles, tm):
    n = x2d.shape[0]
    nt = n // tm
    tok = lambda w: pl.BlockSpec((tm, w), lambda i: (i, 0))
    tab = pl.BlockSpec((tm, LANES), lambda i: (i % seq_tiles, 0))
    full = lambda a: pl.BlockSpec(a.shape, lambda i: (0,) * a.ndim)
    f32, bf16 = jnp.float32, jnp.bfloat16
    sds = jax.ShapeDtypeStruct
    ins = [x2d, cos_t, sin_t, wts['g_attn'], wts['w_in'], wts['g_q'], wts['w_q'], wts['g_kv'],
           wts['w_ukp'], wts['w_uvt']]
    in_specs = [tok(D_MODEL), tab, tab] + [full(a) for a in ins[3:]]
    if sample:
        ins += [wts['w_ukt'], wts['sel']]
        in_specs += [full(wts['w_ukt']), full(wts['sel'])]
        outs = [sds((n, KV_LORA), f32), sds((n, ROPE), f32), sds((n, SB_WIDTH), f32), sds((n, SB_WIDTH), f32),
                sds((n, SB_WIDTH), bf16), sds((n, N_HEADS * KV_LORA), bf16), sds((n, N_HEADS * ROPE), bf16)]
        out_specs = [tok(KV_LORA), tok(ROPE), tok(SB_WIDTH), tok(SB_WIDTH), tok(SB_WIDTH),
                     tok(N_HEADS * KV_LORA), tok(N_HEADS * ROPE)]
    else:
        n_b = nt // seq_tiles
        seq = seq_tiles * tm
        fm_leaf = lambda w: pl.BlockSpec((None, w, tm), lambda i: (i // seq_tiles, 0, i % seq_tiles))
        fm_tile = lambda w: pl.BlockSpec((None, w, tm), lambda i: (i, 0, 0))
        outs = [sds((n, KV_LORA), f32), sds((n_b, ROPE, seq), f32), sds((n_b, SB_WIDTH, seq), f32),
                sds((n_b, SB_WIDTH, seq), f32), sds((nt, QK_WIDTH, tm), bf16), sds((n, QK_WIDTH), bf16),
                sds((nt, MLA_WIDTH, tm), bf16), sds((nt, SB_WIDTH, tm), bf16), sds((n, SB_WIDTH), bf16),
                sds((nt, SB_WIDTH, tm), bf16)]
        out_specs = [tok(KV_LORA), fm_leaf(ROPE), fm_leaf(SB_WIDTH), fm_leaf(SB_WIDTH), fm_tile(QK_WIDTH),
                     tok(QK_WIDTH), fm_tile(MLA_WIDTH), fm_tile(SB_WIDTH), tok(SB_WIDTH), fm_tile(SB_WIDTH)]
    return pl.pallas_call(
        functools.partial(_proj_kernel, sample=sample),
        out_shape=outs, grid=(nt,), in_specs=in_specs, out_specs=out_specs,
        compiler_params=_cparams(("parallel",)),
        name="proj_sample" if sample else "proj_prompt",
    )(*ins)


def _rope_tables(pos):
    inv = ROPE_THETA ** (-jnp.arange(HALF_ROPE, dtype=jnp.float32) / HALF_ROPE)
    ang = pos.astype(jnp.float32)[:, None] * inv[None, :]
    cos, sin = jnp.cos(ang), jnp.sin(ang)
    t = pos.shape[0]
    ones, zeros = jnp.ones((t, NOPE), jnp.float32), jnp.zeros((t, NOPE), jnp.float32)
    pad = jnp.zeros((t, HEAD_PAD - NOPE - ROPE), jnp.float32)
    return (jnp.concatenate([ones, cos, cos, pad], axis=1),
            jnp.concatenate([zeros, -sin, sin, pad], axis=1))


def _prep_weights(g_attn_norm, w_in, g_q_norm, w_q_up, g_kv_norm, w_kv_up):
    bf16 = jnp.bfloat16
    o1, o2, o3 = Q_LORA, Q_LORA + KV_LORA, Q_LORA + KV_LORA + ROPE
    o4, o5 = o3 + SB_WIDTH, o3 + 2 * SB_WIDTH
    kr_slot = jnp.concatenate([jnp.zeros((D_MODEL, NOPE), w_in.dtype), w_in[:, o2:o3],
                               jnp.zeros((D_MODEL, HEAD_PAD - NOPE - ROPE), w_in.dtype)], axis=1)
    w_in_r = jnp.concatenate([w_in[:, :o2], w_in[:, o3:o4], w_in[:, o4:o5], w_in[:, o5:], kr_slot], axis=1)
    w_q_slot = jnp.concatenate([w_q_up, jnp.zeros((Q_LORA, N_HEADS, HEAD_PAD - NOPE - ROPE), w_q_up.dtype)],
                               axis=2).reshape(Q_LORA, QK_WIDTH)
    w_uk = w_kv_up[:, :, :NOPE]
    w_uv = w_kv_up[:, :, NOPE:]
    w_ukp = jnp.concatenate([w_uk, jnp.zeros_like(w_uk)], axis=2).reshape(KV_LORA, QK_WIDTH)
    w_ukt = jnp.concatenate([jnp.transpose(w_uk, (1, 2, 0)),
                             jnp.zeros((N_HEADS, HEAD_PAD - NOPE, KV_LORA), w_uk.dtype)], axis=1)
    r = jnp.arange(QK_WIDTH)
    c = jnp.arange(N_HEADS * ROPE)
    sel = ((r[:, None] // HEAD_PAD == c[None, :] // ROPE)
           & (r[:, None] % HEAD_PAD == NOPE + c[None, :] % ROPE)).astype(bf16)
    return dict(
        g_attn=g_attn_norm.reshape(1, D_MODEL), w_in=w_in_r.astype(bf16), g_q=g_q_norm.reshape(1, Q_LORA),
        w_q=w_q_slot.astype(bf16), g_kv=g_kv_norm.reshape(1, KV_LORA), w_ukp=w_ukp.astype(bf16),
        w_uvf=w_uv.reshape(KV_LORA, MLA_WIDTH).astype(bf16),
        w_uvt=w_uv.reshape(KV_LORA, MLA_WIDTH).T.astype(bf16), w_ukt=w_ukt.astype(bf16), sel=sel)


def _store_heads(o_ref, outs_t):
    for pair in range(N_HEADS // 2):
        both = jnp.concatenate([outs_t[2 * pair], outs_t[2 * pair + 1]], axis=0)
        o_ref[:, pair * LANES:(pair + 1) * LANES] = both.T


def _mla_prompt_kernel(qt_ref, k_ref, vt_ref, o_ref, *, tq):
    qi = pl.program_id(1)
    f32, bf16 = jnp.float32, jnp.bfloat16
    key = lax.broadcasted_iota(jnp.int32, (tq, tq), 0)
    qry = lax.broadcasted_iota(jnp.int32, (tq, tq), 1)
    causal = key <= qry

    def scores(ks, h):
        k = k_ref[pl.ds(ks, tq), h * HEAD_PAD:(h + 1) * HEAD_PAD]
        return _dot(k, qt_ref[h * HEAD_PAD:(h + 1) * HEAD_PAD, :])

    def step(j, carry, masked):
        ks = pl.multiple_of(j * tq, tq)
        new = [None] * N_HEADS
        ahead = [scores(ks, 0), scores(ks, 1)]
        pending = None
        for h in range(N_HEADS):
            m, l, acc = carry[h]
            s = ahead.pop(0)
            if h + 2 < N_HEADS:
                ahead.append(scores(ks, h + 2))
            if masked:
                s = jnp.where(causal, s, NEG_BIG)
            m_new = jnp.maximum(m, jnp.max(s, axis=0, keepdims=True))
            p = jnp.exp2(s - m_new)
            alpha = jnp.exp2(m - m_new)
            l = alpha * l + jnp.sum(p, axis=0, keepdims=True)
            pv = _dot(vt_ref[j, h * HEAD_DIM:(h + 1) * HEAD_DIM, :], p.astype(bf16))
            if pending is not None:
                g, m_g, l_g, alpha_g, acc_g, pv_g = pending
                new[g] = (m_g, l_g, alpha_g * acc_g + pv_g)
            pending = (h, m_new, l, alpha, acc, pv)
        g, m_g, l_g, alpha_g, acc_g, pv_g = pending
        new[g] = (m_g, l_g, alpha_g * acc_g + pv_g)
        return tuple(new)

    init = tuple((jnp.full((1, tq), NEG_BIG, f32), jnp.zeros((1, tq), f32), jnp.zeros((HEAD_DIM, tq), f32))
                 for _ in range(N_HEADS))
    carry = lax.fori_loop(0, qi, functools.partial(step, masked=False), init)
    carry = step(qi, carry, True)
    _store_heads(o_ref, [acc / l for (_, l, acc) in carry])


def _sb_prompt_kernel(sqt_ref, k_ref, vt_ref, o_ref, *, tq):
    qi = pl.program_id(1)
    f32, bf16 = jnp.float32, jnp.bfloat16
    key = lax.broadcasted_iota(jnp.int32, (tq, tq), 0)
    qry = lax.broadcasted_iota(jnp.int32, (tq, tq), 1)
    strict = key < qry
    later = (qry > key).astype(bf16)
    zeros = jnp.zeros((HEAD_DIM, tq), bf16)

    def q_slot(h):
        qh = sqt_ref[h * HEAD_DIM:(h + 1) * HEAD_DIM, :]
        return jnp.concatenate([qh, zeros] if h % 2 == 0 else [zeros, qh], axis=0)

    def logits(ks, h):
        k2 = k_ref[pl.ds(ks, tq), (h // 2) * LANES:(h // 2 + 1) * LANES]
        return _dot(k2, q_slot(h))

    def step(j, carry, masked):
        ks = pl.multiple_of(j * tq, tq)
        new = [None] * N_HEADS

        def stage2(h, z, sp, cum):
            r, acc = carry[h]
            a = jnp.exp(z - sp - cum - r)
            if masked:
                a = jnp.where(strict, a, 0.0)
            pv = _dot(vt_ref[j, h * HEAD_DIM:(h + 1) * HEAD_DIM, :], a.astype(bf16))
            return h, r + jnp.sum(sp, axis=0, keepdims=True), acc, pv

        z_next = logits(ks, 0)
        mid = None
        tail = None
        for h in range(N_HEADS):
            z = z_next
            if h + 1 < N_HEADS:
                z_next = logits(ks, h + 1)
            sp = jnp.maximum(z, 0.0) + jnp.log(1.0 + jnp.exp(-jnp.abs(z)))
            if masked:
                sp = jnp.where(strict, sp, 0.0)
            hi, lo = _split_bf16(sp)
            cum = _dot(later, hi) + _dot(later, lo)
            if mid is not None:
                done = stage2(*mid)
                if tail is not None:
                    g, r_g, acc_g, pv_g = tail
                    new[g] = (r_g, acc_g + pv_g)
                tail = done
            mid = (h, z, sp, cum)
        done = stage2(*mid)
        for g, r_g, acc_g, pv_g in (tail, done):
            new[g] = (r_g, acc_g + pv_g)
        return tuple(new)

    def all_dead(carry):
        lowest = carry[0][0]
        for h in range(1, N_HEADS):
            lowest = jnp.minimum(lowest, carry[h][0])
        return jnp.min(lowest) > SB_DEAD

    init = tuple((jnp.zeros((1, tq), f32), jnp.zeros((HEAD_DIM, tq), f32)) for _ in range(N_HEADS))
    carry = step(qi, init, True)

    def cond(state):
        i, dead, _ = state
        return jnp.logical_and(i < qi, jnp.logical_not(dead))

    def body(state):
        i, _, carry = state
        carry = step(qi - 1 - i, carry, False)
        return i + 1, all_dead(carry), carry

    _, _, carry = lax.while_loop(cond, body, (jnp.int32(0), all_dead(carry), carry))
    _store_heads(o_ref, [acc for (_, acc) in carry])


def _prompt_attn_call(kern, qt, k, vt, *, batch, seq, tq, name):
    nq = seq // tq
    return pl.pallas_call(
        functools.partial(kern, tq=tq),
        out_shape=jax.ShapeDtypeStruct((batch * seq, N_HEADS * HEAD_DIM), jnp.float32),
        grid=(batch, nq),
        in_specs=[pl.BlockSpec((None, qt.shape[1], tq), lambda b, i: (b * nq + i, 0, 0)),
                  pl.BlockSpec((seq, k.shape[1]), lambda b, i: (b, 0)),
                  pl.BlockSpec((nq, vt.shape[1], tq), lambda b, i: (b, 0, 0))],
        out_specs=pl.BlockSpec((tq, N_HEADS * HEAD_DIM), lambda b, i: (b * nq + i, 0)),
        compiler_params=_cparams(("parallel", "arbitrary")),
        name=name,
    )(qt, k, vt)


N_QH = 64


def _softplus(z):
    return jnp.maximum(z, 0.0) + jnp.log1p(jnp.exp(-jnp.abs(z)))


def _split_bf16(x):
    hi = x.astype(jnp.bfloat16)
    return hi, (x - hi.astype(jnp.float32)).astype(jnp.bfloat16)


def _own_head(full):
    rowi = lax.broadcasted_iota(jnp.int32, full.shape, 0)
    lanei = lax.broadcasted_iota(jnp.int32, full.shape, 1)
    kept = jnp.where(lanei // HEAD_DIM == rowi % N_HEADS, full, 0.0)
    return jnp.sum(kept.reshape(N_QH // N_HEADS, N_HEADS, full.shape[1]), axis=1)


def _pad_rows(x, rows):
    return jnp.concatenate([x, jnp.zeros((rows - x.shape[0], x.shape[1]), x.dtype)], axis=0)


SB_SUB = 2


def _sample_attn_kernel(pt_ref, qabs_ref, qrope_ref, sq_ref, cn_ref, krn_ref, skn_ref, svn_ref, wuv_ref, *rest,
                        n_pg, n_pages):
    ckv_pg = rest[0:n_pg]
    kr_pg = rest[n_pg:2 * n_pg]
    (sbk_hbm, sbv_hbm, o_mla_ref, o_sb_ref,
     m_sc, l_sc, acc_sc, r_sc, accsb_sc, qbd_sc, kbuf, vbuf, sems) = rest[2 * n_pg:]
    seq = pl.program_id(0)
    c = pl.program_id(1)
    n_seq = pl.num_programs(0)
    n_sub = n_pages // SB_SUB
    bf16 = jnp.bfloat16

    def sb_copies(s_idx, sub, slot):
        out = []
        for p in range(SB_SUB):
            page = pt_ref[s_idx, n_pages - 1 - (sub * SB_SUB + p)]
            out.append(pltpu.make_async_copy(sbk_hbm.at[page], kbuf.at[slot, p], sems.at[slot]))
            out.append(pltpu.make_async_copy(sbv_hbm.at[page], vbuf.at[slot, p], sems.at[slot]))
        return out
    qabs = qabs_ref[...]
    qrope = qrope_ref[...]

    later = (lax.broadcasted_iota(jnp.int32, (PAGE, PAGE), 0)
             > lax.broadcasted_iota(jnp.int32, (PAGE, PAGE), 1)).astype(bf16)

    def suffix_in_block(lg):
        hi, lo = _split_bf16(lg)
        both = _dot(jnp.concatenate([hi, lo], axis=0), later)
        return both[:N_QH] + both[N_QH:]

    @pl.when(c == 0)
    def _new_rows():
        sq16 = _pad_rows(sq_ref[...], 16).astype(bf16)
        pick = (lax.broadcasted_iota(jnp.int32, (N_QH, 16), 0) // N_HEADS
                == lax.broadcasted_iota(jnp.int32, (N_QH, 16), 1)).astype(bf16)
        rep = _dot(pick, sq16)
        rowi = lax.broadcasted_iota(jnp.int32, rep.shape, 0)
        lanei = lax.broadcasted_iota(jnp.int32, rep.shape, 1)
        qbd = jnp.where(lanei // HEAD_DIM == rowi % N_HEADS, rep, 0.0).astype(bf16)
        qbd_sc[...] = qbd

        key = lax.broadcasted_iota(jnp.int32, (N_QH, PAGE), 1)
        qry = lax.broadcasted_iota(jnp.int32, (N_QH, PAGE), 0) // N_HEADS
        cn = _pad_rows(cn_ref[...], PAGE).astype(bf16)
        krn = _pad_rows(krn_ref[...], PAGE).astype(bf16)
        s = _dot_nt(qabs, cn) + _dot_nt(qrope, krn)
        s = jnp.where(key <= qry, s, NEG_BIG)
        m = jnp.max(s, axis=1, keepdims=True)
        p = jnp.exp2(s - m)
        m_sc[...] = m
        l_sc[...] = jnp.sum(p, axis=1, keepdims=True)
        acc_sc[...] = _dot(p.astype(bf16), cn)

        strict = key < qry
        z = _dot_nt(qbd, _pad_rows(skn_ref[...], PAGE).astype(bf16))
        lg = jnp.where(strict, -_softplus(z), 0.0)
        a = jnp.where(strict, jnp.exp(z + lg + suffix_in_block(lg)), 0.0)
        accsb_sc[...] = _dot(a.astype(bf16), _pad_rows(svn_ref[...], PAGE).astype(bf16))
        r_sc[...] = jnp.sum(lg, axis=1, keepdims=True)

        @pl.when(seq == 0)
        def _():
            for cp in sb_copies(0, 0, 0):
                cp.start()

        def dead_now():
            return jnp.max(r_sc[...]) < -SB_DEAD

        def sb_round(state):
            i, _ = state
            slot = i % 2
            for cp in sb_copies(seq, i, slot):
                cp.wait()

            @pl.when(i + 1 < n_sub)
            def _():
                for cp in sb_copies(seq, i + 1, 1 - slot):
                    cp.start()

            r = r_sc[...]
            sb = jnp.zeros((N_QH, SB_WIDTH), jnp.float32)
            for p in range(SB_SUB):
                z = _dot(qbd, kbuf[slot, p].astype(bf16))
                lg = -_softplus(z)
                a = jnp.exp(z + lg + suffix_in_block(lg) + r)
                sb = sb + _dot_nt(a.astype(bf16), vbuf[slot, p].astype(bf16))
                r = r + jnp.sum(lg, axis=1, keepdims=True)
            r_sc[...] = r
            accsb_sc[...] = accsb_sc[...] + sb
            return i + 1, dead_now()

        n_done, _ = lax.while_loop(lambda st: jnp.logical_and(st[0] < n_sub, jnp.logical_not(st[1])),
                                   sb_round, (jnp.int32(0), dead_now()))

        @pl.when(n_done < n_sub)
        def _():
            for cp in sb_copies(seq, n_done, n_done % 2):
                cp.wait()

        @pl.when(seq + 1 < n_seq)
        def _():
            for cp in sb_copies(seq + 1, 0, 0):
                cp.start()

    ck = [ckv_pg[p][...].astype(bf16) for p in range(n_pg)]
    s_pg = [_dot_nt(qabs, ck[p]) + _dot(qrope, kr_pg[p][...].astype(bf16)) for p in range(n_pg)]
    m_old = m_sc[...]
    m_new = m_old
    for p in range(n_pg):
        m_new = jnp.maximum(m_new, jnp.max(s_pg[p], axis=1, keepdims=True))
    alpha = jnp.exp2(m_old - m_new)
    l_new = alpha * l_sc[...]
    pv = jnp.zeros((N_QH, KV_LORA), jnp.float32)
    for p in range(n_pg):
        e = jnp.exp2(s_pg[p] - m_new)
        l_new = l_new + jnp.sum(e, axis=1, keepdims=True)
        pv = pv + _dot(e.astype(bf16), ck[p])
    m_sc[...] = m_new
    l_sc[...] = l_new
    acc_sc[...] = acc_sc[...] * alpha + pv

    @pl.when(c == pl.num_programs(1) - 1)
    def _finish():
        o_lat = acc_sc[...] / l_sc[...]
        o_mla_ref[...] = _own_head(_dot(o_lat.astype(bf16), wuv_ref[...]))
        o_sb_ref[...] = _own_head(accsb_sc[...])


def _feature_major(cache):
    return jnp.moveaxis(cache, 1, -1).reshape(cache.shape[0], -1, PAGE)


def _sample_attn_call(page_table, qabs, qrope, sq, ckv_n, kr_n, sk_n, sv_n, w_uvf, cache_ckv, cache_krope,
                      cache_sb_k, cache_sb_v, *, n_pg):
    n_seq, n_pages = page_table.shape
    n_chunks = n_pages // n_pg
    t_new = ckv_n.shape[0] // n_seq

    def seq_blk(shape):
        return pl.BlockSpec((None,) + shape, lambda s, c, pt: (s, 0, 0))

    def new_blk(w):
        return pl.BlockSpec((t_new, w), lambda s, c, pt: (s, 0))

    def page_blk(shape, p):
        return pl.BlockSpec((None,) + shape, lambda s, c, pt: (pt[s, n_pages - 1 - (c * n_pg + p)], 0, 0))

    in_specs = [seq_blk((N_QH, KV_LORA)), seq_blk((N_QH, ROPE)), new_blk(SB_WIDTH), new_blk(KV_LORA),
                new_blk(ROPE), new_blk(SB_WIDTH), new_blk(SB_WIDTH),
                pl.BlockSpec(w_uvf.shape, lambda s, c, pt: (0, 0))]
    args = [qabs.reshape(n_seq, N_QH, KV_LORA), qrope.reshape(n_seq, N_QH, ROPE), sq, ckv_n, kr_n, sk_n, sv_n, w_uvf]
    for arr, shape in ((cache_ckv, (PAGE, KV_LORA)), (cache_krope, (ROPE, PAGE))):
        for p in range(n_pg):
            in_specs.append(page_blk(shape, p))
            args.append(arr)
    in_specs += [pl.BlockSpec(memory_space=pl.ANY), pl.BlockSpec(memory_space=pl.ANY)]
    args += [cache_sb_k, cache_sb_v]
    out_blk = pl.BlockSpec((t_new, MLA_WIDTH), lambda s, c, pt: (s, 0))
    f32 = jnp.float32
    return pl.pallas_call(
        functools.partial(_sample_attn_kernel, n_pg=n_pg, n_pages=n_pages),
        out_shape=[jax.ShapeDtypeStruct((n_seq * t_new, MLA_WIDTH), f32),
                   jax.ShapeDtypeStruct((n_seq * t_new, SB_WIDTH), f32)],
        grid_spec=pltpu.PrefetchScalarGridSpec(
            num_scalar_prefetch=1, grid=(n_seq, n_chunks), in_specs=in_specs, out_specs=[out_blk, out_blk],
            scratch_shapes=[pltpu.VMEM((N_QH, 1), f32), pltpu.VMEM((N_QH, 1), f32), pltpu.VMEM((N_QH, KV_LORA), f32),
                            pltpu.VMEM((N_QH, 1), f32), pltpu.VMEM((N_QH, SB_WIDTH), f32),
                            pltpu.VMEM((N_QH, SB_WIDTH), jnp.bfloat16),
                            pltpu.VMEM((2, SB_SUB, SB_WIDTH, PAGE), f32), pltpu.VMEM((2, SB_SUB, SB_WIDTH, PAGE), f32),
                            pltpu.SemaphoreType.DMA((2,))]),
        compiler_params=_cparams(("arbitrary", "arbitrary")),
        name="sample_attn",
    )(page_table, *args)


def _lane_pack(cols):
    lane = lax.broadcasted_iota(jnp.int32, (cols[0].shape[0], LANES), 1)
    out = jnp.zeros((cols[0].shape[0], LANES), jnp.float32)
    for k, col in enumerate(cols):
        out = jnp.where(lane == k, col, out)
    return out


def _out_router_kernel(omla_ref, osb_ref, x_ref, gm_ref, gs_ref, wout_ref, gmoe_ref, wr_hi_ref, wr_lo_ref, br_ref,
                       h_ref, tok_ref, eidx_ref, gate_ref, rank_ref, cnt_ref, carry_sc, *, tm):
    bf16 = jnp.bfloat16

    @pl.when(pl.program_id(0) == 0)
    def _():
        carry_sc[...] = jnp.zeros_like(carry_sc)

    o = jnp.concatenate([_rms(omla_ref[...], gm_ref[...]), _rms(osb_ref[...], gs_ref[...])], axis=1).astype(bf16)
    h = x_ref[...] + _dot(o, wout_ref[...])
    h_ref[...] = h
    tok = _rms(h, gmoe_ref[...])
    tok_ref[...] = tok
    t_hi, t_lo = _split_bf16(tok)
    wr_hi = wr_hi_ref[...]
    logits = _dot(t_hi, wr_hi) + (_dot(t_hi, wr_lo_ref[...]) + _dot(t_lo, wr_hi)) + br_ref[...]

    lane = lax.broadcasted_iota(jnp.int32, (tm, LANES), 1).astype(jnp.float32)
    work = logits
    vals, idxs, sels = [], [], []
    for _ in range(TOP_K):
        v = jnp.max(work, axis=1, keepdims=True)
        idx = jnp.min(jnp.where(work == v, lane, float(LANES)), axis=1, keepdims=True)
        sel = lane == idx
        work = jnp.where(sel, -jnp.inf, work)
        vals.append(v)
        idxs.append(idx)
        sels.append(sel)
    exps = [jnp.exp(v - vals[0]) for v in vals]
    denom = exps[0] + exps[1] + exps[2] + exps[3]
    gates = [e / denom for e in exps]

    assign = jnp.zeros((tm, LANES), jnp.float32)
    for sel in sels:
        assign = jnp.where(sel, 1.0, assign)
    earlier = (lax.broadcasted_iota(jnp.int32, (tm, tm), 1)
               < lax.broadcasted_iota(jnp.int32, (tm, tm), 0)).astype(bf16)
    prefix = _dot(earlier, assign.astype(bf16)) + carry_sc[...]
    ranks = [jnp.sum(jnp.where(sel, prefix, 0.0), axis=1, keepdims=True) for sel in sels]
    carry = carry_sc[...] + jnp.sum(assign, axis=0, keepdims=True)
    carry_sc[...] = carry
    cnt_ref[...] = carry
    eidx_ref[...] = _lane_pack(idxs)
    gate_ref[...] = _lane_pack(gates)
    rank_ref[...] = _lane_pack(ranks)


def _out_router_call(o_mla, o_sb, x2d, wts, *, tm):
    n = x2d.shape[0]
    tok = lambda w: pl.BlockSpec((tm, w), lambda i: (i, 0))
    full = lambda a: pl.BlockSpec(a.shape, lambda i: (0,) * a.ndim)
    f32 = jnp.float32
    sds = jax.ShapeDtypeStruct
    ws = [wts['g_mla_out'], wts['g_sb_out'], wts['w_out'], wts['g_moe'], wts['wr_hi'], wts['wr_lo'], wts['b_router']]
    return pl.pallas_call(
        functools.partial(_out_router_kernel, tm=tm),
        out_shape=[sds((n, D_MODEL), f32), sds((n, D_MODEL), f32), sds((n, LANES), f32), sds((n, LANES), f32),
                   sds((n, LANES), f32), sds((1, LANES), f32)],
        grid=(n // tm,),
        in_specs=[tok(MLA_WIDTH), tok(SB_WIDTH), tok(D_MODEL)] + [full(a) for a in ws],
        out_specs=[tok(D_MODEL), tok(D_MODEL), tok(LANES), tok(LANES), tok(LANES),
                   pl.BlockSpec((1, LANES), lambda i: (0, 0))],
        scratch_shapes=[pltpu.VMEM((1, LANES), f32)],
        compiler_params=_cparams(("arbitrary",)),
        name="out_router",
    )(o_mla, o_sb, x2d, *ws)


def _row_copy(src, s, dst, d, sem):
    return pltpu.make_async_copy(src.at[pl.ds(s, 1), :], dst.at[pl.ds(d, 1), :], sem)


def _dispatch_kernel(dest_ref, tokp_ref, toks_ref, xs_ref, sem, *, tm, n_prompt_tiles):
    i = pl.program_id(0)

    def scatter(tok_ref):
        def start(t, _):
            for k in range(TOP_K):
                _row_copy(tok_ref, t, xs_ref, dest_ref[0, t * TOP_K + k], sem).start()
            return 0

        def wait(t, _):
            for k in range(TOP_K):
                _row_copy(tok_ref, 0, xs_ref, 0, sem).wait()
            return 0

        lax.fori_loop(0, tm, start, 0)
        lax.fori_loop(0, tm, wait, 0)

    @pl.when(i < n_prompt_tiles)
    def _():
        scatter(tokp_ref)

    @pl.when(i >= n_prompt_tiles)
    def _():
        scatter(toks_ref)


def _dispatch_call(dest_tiles, tok_p, tok_s, *, tm):
    npt, nst = tok_p.shape[0] // tm, tok_s.shape[0] // tm
    n_pairs = (tok_p.shape[0] + tok_s.shape[0]) * TOP_K
    return pl.pallas_call(
        functools.partial(_dispatch_kernel, tm=tm, n_prompt_tiles=npt),
        out_shape=jax.ShapeDtypeStruct((n_pairs, D_MODEL), jnp.float32),
        grid=(npt + nst,),
        in_specs=[pl.BlockSpec((None, 1, tm * TOP_K), lambda i: (i, 0, 0), memory_space=pltpu.SMEM),
                  pl.BlockSpec((tm, D_MODEL), lambda i: (jnp.minimum(i, npt - 1), 0)),
                  pl.BlockSpec((tm, D_MODEL), lambda i: (jnp.maximum(i - npt, 0), 0))],
        out_specs=pl.BlockSpec(memory_space=pl.ANY),
        scratch_shapes=[pltpu.SemaphoreType.DMA],
        compiler_params=_cparams(("arbitrary",)),
        name="moe_dispatch",
    )(dest_tiles, tok_p, tok_s)


def _expert_kernel(blk_ref, exp_ref, lo_ref, hi_ref, first_ref, xs_ref, wup_ref, bup_ref, wdn_ref, bdn_ref,
                   y_ref, wup_sc, wdn_sc, *, bm):
    i = pl.program_id(0)
    bf16 = jnp.bfloat16
    lo, hi = lo_ref[i], hi_ref[i]
    e_now = exp_ref[i]
    e_prev = exp_ref[jnp.maximum(i - 1, 0)]

    @pl.when((i == 0) | (e_now != e_prev))
    def _():
        wup_sc[...] = wup_ref[...].astype(bf16)
        wdn_sc[...] = wdn_ref[...].astype(bf16)

    @pl.when(hi > lo)
    def _():
        x = xs_ref[...].astype(bf16)
        hcat = _dot(x, wup_sc[...]) + bup_ref[...]
        x_glu = jnp.minimum(hcat[:, :D_FF], SWIGLU_LIMIT)
        x_lin = jnp.clip(hcat[:, D_FF:], -SWIGLU_LIMIT, SWIGLU_LIMIT)
        act = x_glu * jax.nn.sigmoid(SWIGLU_ALPHA * x_glu) * (x_lin + 1.0)
        y = _dot(act.astype(bf16), wdn_sc[...]) + bdn_ref[...]
        rows = lax.broadcasted_iota(jnp.int32, (bm, 1), 0)
        y = jnp.where((rows >= lo) & (rows < hi), y, 0.0)

        @pl.when(first_ref[i] == 1)
        def _():
            y_ref[...] = y

        @pl.when(first_ref[i] == 0)
        def _():
            y_ref[...] = y_ref[...] + y


def _expert_call(items, xs, w_up, b_up, w_dn, b_dn, *, bm):
    n_items = items[0].shape[0]
    return pl.pallas_call(
        functools.partial(_expert_kernel, bm=bm),
        out_shape=jax.ShapeDtypeStruct(xs.shape, jnp.float32),
        grid_spec=pltpu.PrefetchScalarGridSpec(
            num_scalar_prefetch=5, grid=(n_items,),
            in_specs=[pl.BlockSpec((bm, D_MODEL), lambda i, blk, ex, lo, hi, fi: (blk[i], 0)),
                      pl.BlockSpec((None, D_MODEL, 2 * D_FF), lambda i, blk, ex, lo, hi, fi: (ex[i], 0, 0)),
                      pl.BlockSpec((None, 1, 2 * D_FF), lambda i, blk, ex, lo, hi, fi: (ex[i], 0, 0)),
                      pl.BlockSpec((None, D_FF, D_MODEL), lambda i, blk, ex, lo, hi, fi: (ex[i], 0, 0)),
                      pl.BlockSpec((None, 1, D_MODEL), lambda i, blk, ex, lo, hi, fi: (ex[i], 0, 0))],
            out_specs=pl.BlockSpec((bm, D_MODEL), lambda i, blk, ex, lo, hi, fi: (blk[i], 0)),
            scratch_shapes=[pltpu.VMEM((D_MODEL, 2 * D_FF), jnp.bfloat16), pltpu.VMEM((D_FF, D_MODEL), jnp.bfloat16)]),
        compiler_params=_cparams(("arbitrary",)),
        name="moe_experts",
    )(*items, xs, w_up, b_up.reshape(N_EXPERTS, 1, 2 * D_FF), w_dn, b_dn.reshape(N_EXPERTS, 1, D_MODEL))


def _combine_kernel(dest_ref, h_ref, gate_ref, gf_ref, ys_ref, out_ref, buf, sem, *, tm):
    def start(t, _):
        for k in range(TOP_K):
            pltpu.make_async_copy(ys_ref.at[pl.ds(dest_ref[0, t * TOP_K + k], 1), :],
                                  buf.at[k, pl.ds(t, 1), :], sem).start()
        return 0

    def wait(t, _):
        for k in range(TOP_K):
            pltpu.make_async_copy(ys_ref.at[pl.ds(0, 1), :], buf.at[0, pl.ds(0, 1), :], sem).wait()
        return 0

    lax.fori_loop(0, tm, start, 0)
    lax.fori_loop(0, tm, wait, 0)
    gate = gate_ref[...]
    h = h_ref[...]
    for k in range(TOP_K):
        h = h + gate[:, k:k + 1] * buf[k]
    out_ref[...] = _rms(h, gf_ref[...])


def _combine_call(dest_tiles, h, gate, g_final, ys, *, tm):
    n = h.shape[0]
    return pl.pallas_call(
        functools.partial(_combine_kernel, tm=tm),
        out_shape=jax.ShapeDtypeStruct((n, D_MODEL), jnp.float32),
        grid=(n // tm,),
        in_specs=[pl.BlockSpec((None, 1, tm * TOP_K), lambda i: (i, 0, 0), memory_space=pltpu.SMEM),
                  pl.BlockSpec((tm, D_MODEL), lambda i: (i, 0)),
                  pl.BlockSpec((tm, LANES), lambda i: (i, 0)),
                  pl.BlockSpec((1, D_MODEL), lambda i: (0, 0)),
                  pl.BlockSpec(memory_space=pl.ANY)],
        out_specs=pl.BlockSpec((tm, D_MODEL), lambda i: (i, 0)),
        scratch_shapes=[pltpu.VMEM((TOP_K, tm, D_MODEL), jnp.float32), pltpu.SemaphoreType.DMA],
        compiler_params=_cparams(("arbitrary",)),
        name="moe_combine",
    )(dest_tiles, h, gate, g_final.reshape(1, D_MODEL), ys)


def _plan_items(counts, n_pairs, bm):
    n_blocks = n_pairs // bm
    n_items = n_blocks + N_EXPERTS - 1
    ends = jnp.cumsum(counts)
    starts = ends - counts
    blk_lo = jnp.arange(n_blocks, dtype=jnp.int32) * bm
    first_blk = starts // bm
    last_blk = jnp.where(counts > 0, (ends - 1) // bm, first_blk - 1)
    n_touch = jnp.maximum(last_blk - first_blk + 1, 0)
    item_end = jnp.cumsum(n_touch)
    item_start = item_end - n_touch
    total = item_end[-1]
    it = jnp.arange(n_items, dtype=jnp.int32)
    e_of = jnp.minimum(jnp.sum(it[:, None] >= item_end[None, :], axis=1), N_EXPERTS - 1).astype(jnp.int32)
    blk_of = first_blk[e_of] + (it - item_start[e_of])
    valid = it < total
    last_e = e_of[jnp.maximum(total - 1, 0)]
    last_b = blk_of[jnp.maximum(total - 1, 0)]
    e_of = jnp.where(valid, e_of, last_e)
    blk_of = jnp.where(valid, blk_of, last_b).astype(jnp.int32)
    lo = jnp.clip(starts[e_of] - blk_of * bm, 0, bm)
    hi = jnp.clip(ends[e_of] - blk_of * bm, 0, bm)
    lo = jnp.where(valid, lo, 0).astype(jnp.int32)
    hi = jnp.where(valid, hi, 0).astype(jnp.int32)
    prev_blk = jnp.concatenate([jnp.full((1,), -1, jnp.int32), blk_of[:-1]])
    first = (valid & (blk_of != prev_blk)).astype(jnp.int32)
    del blk_lo
    return blk_of, e_of.astype(jnp.int32), lo, hi, first


def kernel(x_prompt, x_sample, cache_ckv, cache_krope, cache_sb_k, cache_sb_v, page_table, g_attn_norm, w_in,
           g_q_norm, w_q_up, g_kv_norm, w_kv_up, g_mla_out, g_sb_out, w_out, g_moe_norm, w_router, b_router,
           w_moe_up, b_moe_up, w_moe_down, b_moe_down, g_final):
    b_p, s_p, d = x_prompt.shape
    b_s, s_s, _ = x_sample.shape
    n_p, n_s = b_p * s_p, b_s * s_s
    n_pool = cache_ckv.shape[1]
    past_len = page_table.shape[1] * PAGE
    bf16 = jnp.bfloat16
    wts = _prep_weights(g_attn_norm[0], w_in[0], g_q_norm[0], w_q_up[0], g_kv_norm[0], w_kv_up[0])
    w_r = jnp.pad(w_router[0], ((0, 0), (0, LANES - N_EXPERTS)))
    wr_hi = w_r.astype(bf16)
    wts.update(
        g_mla_out=g_mla_out[0].reshape(1, MLA_WIDTH), g_sb_out=g_sb_out[0].reshape(1, SB_WIDTH),
        w_out=w_out[0].astype(bf16), g_moe=g_moe_norm[0].reshape(1, D_MODEL), wr_hi=wr_hi,
        wr_lo=(w_r - wr_hi.astype(jnp.float32)).astype(bf16),
        b_router=jnp.pad(b_router[0], (0, LANES - N_EXPERTS), constant_values=NEG_BIG).reshape(1, LANES))
    x_p2 = x_prompt.reshape(n_p, d)
    x_s2 = x_sample.reshape(n_s, d)
    tm = 256

    cos_p, sin_p = _rope_tables(jnp.arange(s_p))
    (ckv_p, krt_p, skt_p, svt_p, qt_p, k_p, vt_p, sqt_p, skb_p, svtb_p) = _proj_call(
        x_p2, cos_p, sin_p, wts, sample=False, seq_tiles=s_p // tm, tm=tm)
    o_mla_p = _prompt_attn_call(_mla_prompt_kernel, qt_p, k_p, vt_p, batch=b_p, seq=s_p, tq=tm, name="mla_prompt")
    o_sb_p = _prompt_attn_call(_sb_prompt_kernel, sqt_p, skb_p, svtb_p, batch=b_p, seq=s_p, tq=tm, name="sb_prompt")
    h_p, tok_p, eidx_p, gate_p, rank_p, cnt_p = _out_router_call(o_mla_p, o_sb_p, x_p2, wts, tm=tm)

    cos_s, sin_s = _rope_tables(past_len + jnp.arange(s_s))
    cos_s, sin_s = jnp.tile(cos_s, (b_s, 1)), jnp.tile(sin_s, (b_s, 1))
    (ckv_s, kr_s, sk_s, sv_s, sq_s, qabs_s, qrope_s) = _proj_call(
        x_s2, cos_s, sin_s, wts, sample=True, seq_tiles=n_s // tm, tm=tm)
    o_mla_s, o_sb_s = _sample_attn_call(
        page_table, qabs_s, qrope_s, sq_s.astype(jnp.float32), ckv_s, kr_s, sk_s, sv_s, wts['w_uvf'],
        cache_ckv[0], _feature_major(cache_krope[0]), _feature_major(cache_sb_k[0]), _feature_major(cache_sb_v[0]),
        n_pg=8)
    h_s, tok_s, eidx_s, gate_s, rank_s, cnt_s = _out_router_call(o_mla_s, o_sb_s, x_s2, wts, tm=tm)

    cnt_p = cnt_p[0, :N_EXPERTS].astype(jnp.int32)
    cnt_s = cnt_s[0, :N_EXPERTS].astype(jnp.int32)
    counts = cnt_p + cnt_s
    pstart = jnp.cumsum(counts) - counts
    e_p = eidx_p[:, :TOP_K].astype(jnp.int32)
    e_s = eidx_s[:, :TOP_K].astype(jnp.int32)
    dest_p = pstart[e_p] + rank_p[:, :TOP_K].astype(jnp.int32)
    dest_s = pstart[e_s] + cnt_p[e_s] + rank_s[:, :TOP_K].astype(jnp.int32)
    dest_p = dest_p.reshape(n_p // tm, 1, tm * TOP_K)
    dest_s = dest_s.reshape(n_s // tm, 1, tm * TOP_K)
    bm = 256
    items = _plan_items(counts, (n_p + n_s) * TOP_K, bm)

    xs = _dispatch_call(jnp.concatenate([dest_p, dest_s], axis=0), tok_p, tok_s, tm=tm)
    ys = _expert_call(items, xs, w_moe_up[0], b_moe_up[0], w_moe_down[0], b_moe_down[0], bm=bm)
    y_p = _combine_call(dest_p, h_p, gate_p, g_final, ys, tm=tm)
    y_s = _combine_call(dest_s, h_s, gate_s, g_final, ys, tm=tm)

    return (y_p.reshape(b_p, s_p, d), y_s.reshape(b_s, s_s, d),
            ckv_p.reshape(1, b_p, s_p, KV_LORA), jnp.swapaxes(krt_p, 1, 2)[None],
            jnp.moveaxis(skt_p.reshape(b_p, N_HEADS, HEAD_DIM, s_p), 3, 1)[None],
            jnp.moveaxis(svt_p.reshape(b_p, N_HEADS, HEAD_DIM, s_p), 3, 1)[None],
            ckv_s.reshape(1, b_s, s_s, KV_LORA), kr_s.reshape(1, b_s, s_s, ROPE),
            sk_s.reshape(1, b_s, s_s, N_HEADS, HEAD_DIM), sv_s.reshape(1, b_s, s_s, N_HEADS, HEAD_DIM))
```

```python
import functools
import math

import jax
import jax.numpy as jnp
from jax import lax
from jax.experimental import pallas as pl
from jax.experimental.pallas import tpu as pltpu

D_MODEL = 1024
HEAD_DIM = 64
N_HEADS = 8
NOPE = 64
ROPE = 32
HALF_ROPE = ROPE // 2
Q_LORA = 384
KV_LORA = 256
SB_WIDTH = N_HEADS * HEAD_DIM
MLA_WIDTH = N_HEADS * HEAD_DIM
HEAD_PAD = 128
QK_WIDTH = N_HEADS * HEAD_PAD
MLA_SCALE = 1.0 / math.sqrt(NOPE + ROPE)
SB_SCALE = 1.0 / math.sqrt(HEAD_DIM)
ROPE_THETA = 10000.0
N_EXPERTS = 32
TOP_K = 4
D_FF = 1024
SWIGLU_ALPHA = 1.702
SWIGLU_LIMIT = 7.0
PAGE = 128
EPS = 1e-6
LANES = 128
NEG_BIG = -1e30
LOG2E = 1.4426950408889634
SB_DEAD = 104.0

_O_CQ = 0
_O_CKV = _O_CQ + Q_LORA
_O_SQ = _O_CKV + KV_LORA
_O_SK = _O_SQ + SB_WIDTH
_O_SV = _O_SK + SB_WIDTH
_O_KR = _O_SV + SB_WIDTH
IN_PAD = _O_KR + HEAD_PAD

VMEM_LIMIT = 56 * 1024 * 1024


def _cparams(sem, vmem=VMEM_LIMIT):
    return pltpu.CompilerParams(dimension_semantics=sem, vmem_limit_bytes=vmem)


def _rms(x, g):
    return x * lax.rsqrt(jnp.mean(x * x, axis=-1, keepdims=True) + EPS) * g


def _dot(a, b):
    return jnp.dot(a, b, preferred_element_type=jnp.float32)


def _dot_nt(a, b):
    return lax.dot_general(a, b, (((1,), (1,)), ((), ())), preferred_element_type=jnp.float32)


def _dot_tn(a, b):
    return lax.dot_general(a, b, (((0,), (0,)), ((), ())), preferred_element_type=jnp.float32)


def _rope_slot(t, cos_t, sin_t):
    lane = lax.broadcasted_iota(jnp.int32, t.shape, 1)
    partner = jnp.where(lane < NOPE + HALF_ROPE,
                        pltpu.roll(t, LANES - HALF_ROPE, 1),
                        pltpu.roll(t, HALF_ROPE, 1))
    return t * cos_t + partner * sin_t


def _proj_kernel(x_ref, cos_ref, sin_ref, g_attn_ref, w_in_ref, g_q_ref, w_q_ref, g_kv_ref,
                 w_ukp_ref, w_uvt_ref, *rest, sample):
    bf16 = jnp.bfloat16
    if sample:
        (w_ukt_ref, sel_ref, ckv_ref, kr_ref, sk_ref, sv_ref, sq_ref, qabs_ref, qrope_ref) = rest
    else:
        (ckv_ref, krt_ref, skt_ref, svt_ref, qt_ref, k_ref, vt_ref, sqt_ref, skb_ref, svtb_ref) = rest
    x = x_ref[...]
    xb = _rms(x, g_attn_ref[...]).astype(bf16)
    proj = _dot(xb, w_in_ref[...])
    cos_t = cos_ref[...]
    sin_t = sin_ref[...]

    cqn = _rms(proj[:, _O_CQ:_O_CKV], g_q_ref[...]).astype(bf16)
    q = _dot(cqn, w_q_ref[...])
    c_kv = _rms(proj[:, _O_CKV:_O_SQ], g_kv_ref[...])
    ckv_ref[...] = c_kv
    ckb = c_kv.astype(bf16)
    kr = _rope_slot(proj[:, _O_KR:IN_PAD], cos_t, sin_t)
    sq = proj[:, _O_SQ:_O_SK] * SB_SCALE
    sk = proj[:, _O_SK:_O_SV]
    sv = proj[:, _O_SV:_O_KR]
    q_heads = [_rope_slot(q[:, h * HEAD_PAD:(h + 1) * HEAD_PAD], cos_t, sin_t) * (MLA_SCALE * LOG2E)
               for h in range(N_HEADS)]

    if sample:
        kr_ref[...] = kr[:, NOPE:NOPE + ROPE]
        sk_ref[...] = sk
        sv_ref[...] = sv
        sq_ref[...] = sq.astype(bf16)
        for h in range(N_HEADS):
            qabs_ref[:, h * KV_LORA:(h + 1) * KV_LORA] = _dot(q_heads[h].astype(bf16), w_ukt_ref[h]).astype(bf16)
        qcat = jnp.concatenate([qh.astype(bf16) for qh in q_heads], axis=1)
        qrope_ref[...] = _dot(qcat, sel_ref[...]).astype(bf16)
    else:
        krt_ref[...] = kr.T[NOPE:NOPE + ROPE, :]
        sk_t = sk.T
        sv_t = sv.T
        skt_ref[...] = sk_t
        svt_ref[...] = sv_t
        svtb_ref[...] = sv_t.astype(bf16)
        skb_ref[...] = sk.astype(bf16)
        sqt_ref[...] = sq.T.astype(bf16)
        k_nope = _dot(ckb, w_ukp_ref[...])
        for h in range(N_HEADS):
            sl = slice(h * HEAD_PAD, (h + 1) * HEAD_PAD)
            qt_ref[sl, :] = q_heads[h].T.astype(bf16)
            k_ref[:, sl] = (k_nope[:, sl] + kr).astype(bf16)
        vt_ref[...] = _dot_nt(w_uvt_ref[...], ckb).astype(bf16)


def _proj_call(x2d, cos_t, sin_t, wts, *, sample, seq_tiles, tm):
    n = x2d.shape[0]
    nt = n // tm
    tok = lambda w: pl.BlockSpec((tm, w), lambda i: (i, 0))
    tab = pl.BlockSpec((tm, LANES), lambda i: (i % seq_tiles, 0))
    full = lambda a: pl.BlockSpec(a.shape, lambda i: (0,) * a.ndim)
    f32, bf16 = jnp.float32, jnp.bfloat16
    sds = jax.ShapeDtypeStruct
    ins = [x2d, cos_t, sin_t, wts['g_attn'], wts['w_in'], wts['g_q'], wts['w_q'], wts['g_kv'],
           wts['w_ukp'], wts['w_uvt']]
    in_specs = [tok(D_MODEL), tab, tab] + [full(a) for a in ins[3:]]
    if sample:
        ins += [wts['w_ukt'], wts['sel']]
        in_specs += [full(wts['w_ukt']), full(wts['sel'])]
        outs = [sds((n, KV_LORA), f32), sds((n, ROPE), f32), sds((n, SB_WIDTH), f32), sds((n, SB_WIDTH), f32),
                sds((n, SB_WIDTH), bf16), sds((n, N_HEADS * KV_LORA), bf16), sds((n, N_HEADS * ROPE), bf16)]
        out_specs = [tok(KV_LORA), tok(ROPE), tok(SB_WIDTH), tok(SB_WIDTH), tok(SB_WIDTH),
                     tok(N_HEADS * KV_LORA), tok(N_HEADS * ROPE)]
    else:
        n_b = nt // seq_tiles
        seq = seq_tiles * tm
        fm_leaf = lambda w: pl.BlockSpec((None, w, tm), lambda i: (i // seq_tiles, 0, i % seq_tiles))
        fm_tile = lambda w: pl.BlockSpec((None, w, tm), lambda i: (i, 0, 0))
        outs = [sds((n, KV_LORA), f32), sds((n_b, ROPE, seq), f32), sds((n_b, SB_WIDTH, seq), f32),
                sds((n_b, SB_WIDTH, seq), f32), sds((nt, QK_WIDTH, tm), bf16), sds((n, QK_WIDTH), bf16),
                sds((nt, MLA_WIDTH, tm), bf16), sds((nt, SB_WIDTH, tm), bf16), sds((n, SB_WIDTH), bf16),
                sds((nt, SB_WIDTH, tm), bf16)]
        out_specs = [tok(KV_LORA), fm_leaf(ROPE), fm_leaf(SB_WIDTH), fm_leaf(SB_WIDTH), fm_tile(QK_WIDTH),
                     tok(QK_WIDTH), fm_tile(MLA_WIDTH), fm_tile(SB_WIDTH), tok(SB_WIDTH), fm_tile(SB_WIDTH)]
    return pl.pallas_call(
        functools.partial(_proj_kernel, sample=sample),
        out_shape=outs, grid=(nt,), in_specs=in_specs, out_specs=out_specs,
        compiler_params=_cparams(("parallel",)),
        name="proj_sample" if sample else "proj_prompt",
    )(*ins)


def _rope_tables(pos):
    inv = ROPE_THETA ** (-jnp.arange(HALF_ROPE, dtype=jnp.float32) / HALF_ROPE)
    ang = pos.astype(jnp.float32)[:, None] * inv[None, :]
    cos, sin = jnp.cos(ang), jnp.sin(ang)
    t = pos.shape[0]
    ones, zeros = jnp.ones((t, NOPE), jnp.float32), jnp.zeros((t, NOPE), jnp.float32)
    pad = jnp.zeros((t, HEAD_PAD - NOPE - ROPE), jnp.float32)
    return (jnp.concatenate([ones, cos, cos, pad], axis=1),
            jnp.concatenate([zeros, -sin, sin, pad], axis=1))


def _prep_weights(g_attn_norm, w_in, g_q_norm, w_q_up, g_kv_norm, w_kv_up):
    bf16 = jnp.bfloat16
    o1, o2, o3 = Q_LORA, Q_LORA + KV_LORA, Q_LORA + KV_LORA + ROPE
    o4, o5 = o3 + SB_WIDTH, o3 + 2 * SB_WIDTH
    kr_slot = jnp.concatenate([jnp.zeros((D_MODEL, NOPE), w_in.dtype), w_in[:, o2:o3],
                               jnp.zeros((D_MODEL, HEAD_PAD - NOPE - ROPE), w_in.dtype)], axis=1)
    w_in_r = jnp.concatenate([w_in[:, :o2], w_in[:, o3:o4], w_in[:, o4:o5], w_in[:, o5:], kr_slot], axis=1)
    w_q_slot = jnp.concatenate([w_q_up, jnp.zeros((Q_LORA, N_HEADS, HEAD_PAD - NOPE - ROPE), w_q_up.dtype)],
                               axis=2).reshape(Q_LORA, QK_WIDTH)
    w_uk = w_kv_up[:, :, :NOPE]
    w_uv = w_kv_up[:, :, NOPE:]
    w_ukp = jnp.concatenate([w_uk, jnp.zeros_like(w_uk)], axis=2).reshape(KV_LORA, QK_WIDTH)
    w_ukt = jnp.concatenate([jnp.transpose(w_uk, (1, 2, 0)),
                             jnp.zeros((N_HEADS, HEAD_PAD - NOPE, KV_LORA), w_uk.dtype)], axis=1)
    r = jnp.arange(QK_WIDTH)
    c = jnp.arange(N_HEADS * ROPE)
    sel = ((r[:, None] // HEAD_PAD == c[None, :] // ROPE)
           & (r[:, None] % HEAD_PAD == NOPE + c[None, :] % ROPE)).astype(bf16)
    return dict(
        g_attn=g_attn_norm.reshape(1, D_MODEL), w_in=w_in_r.astype(bf16), g_q=g_q_norm.reshape(1, Q_LORA),
        w_q=w_q_slot.astype(bf16), g_kv=g_kv_norm.reshape(1, KV_LORA), w_ukp=w_ukp.astype(bf16),
        w_uvf=w_uv.reshape(KV_LORA, MLA_WIDTH).astype(bf16),
        w_uvt=w_uv.reshape(KV_LORA, MLA_WIDTH).T.astype(bf16), w_ukt=w_ukt.astype(bf16), sel=sel)


def _store_heads(o_ref, outs_t):
    for pair in range(N_HEADS // 2):
        both = jnp.concatenate([outs_t[2 * pair], outs_t[2 * pair + 1]], axis=0)
        o_ref[:, pair * LANES:(pair + 1) * LANES] = both.T


def _mla_prompt_kernel(qt_ref, k_ref, vt_ref, o_ref, *, tq):
    qi = pl.program_id(1)
    f32, bf16 = jnp.float32, jnp.bfloat16
    key = lax.broadcasted_iota(jnp.int32, (tq, tq), 0)
    qry = lax.broadcasted_iota(jnp.int32, (tq, tq), 1)
    causal = key <= qry

    def scores(ks, h):
        k = k_ref[pl.ds(ks, tq), h * HEAD_PAD:(h + 1) * HEAD_PAD]
        return _dot(k, qt_ref[h * HEAD_PAD:(h + 1) * HEAD_PAD, :])

    def step(j, carry, masked):
        ks = pl.multiple_of(j * tq, tq)
        new = [None] * N_HEADS
        ahead = [scores(ks, 0), scores(ks, 1)]
        pending = None
        for h in range(N_HEADS):
            m, l, acc = carry[h]
            s = ahead.pop(0)
            if h + 2 < N_HEADS:
                ahead.append(scores(ks, h + 2))
            if masked:
                s = jnp.where(causal, s, NEG_BIG)
            m_new = jnp.maximum(m, jnp.max(s, axis=0, keepdims=True))
            p = jnp.exp2(s - m_new)
            alpha = jnp.exp2(m - m_new)
            l = alpha * l + jnp.sum(p, axis=0, keepdims=True)
            pv = _dot(vt_ref[j, h * HEAD_DIM:(h + 1) * HEAD_DIM, :], p.astype(bf16))
            if pending is not None:
                g, m_g, l_g, alpha_g, acc_g, pv_g = pending
                new[g] = (m_g, l_g, alpha_g * acc_g + pv_g)
            pending = (h, m_new, l, alpha, acc, pv)
        g, m_g, l_g, alpha_g, acc_g, pv_g = pending
        new[g] = (m_g, l_g, alpha_g * acc_g + pv_g)
        return tuple(new)

    init = tuple((jnp.full((1, tq), NEG_BIG, f32), jnp.zeros((1, tq), f32), jnp.zeros((HEAD_DIM, tq), f32))
                 for _ in range(N_HEADS))
    carry = lax.fori_loop(0, qi, functools.partial(step, masked=False), init)
    carry = step(qi, carry, True)
    _store_heads(o_ref, [acc / l for (_, l, acc) in carry])


def _sb_prompt_kernel(sqt_ref, k_ref, vt_ref, o_ref, *, tq):
    qi = pl.program_id(1)
    f32, bf16 = jnp.float32, jnp.bfloat16
    key = lax.broadcasted_iota(jnp.int32, (tq, tq), 0)
    qry = lax.broadcasted_iota(jnp.int32, (tq, tq), 1)
    strict = key < qry
    later = (qry > key).astype(bf16)
    zeros = jnp.zeros((HEAD_DIM, tq), bf16)

    def q_slot(h):
        qh = sqt_ref[h * HEAD_DIM:(h + 1) * HEAD_DIM, :]
        return jnp.concatenate([qh, zeros] if h % 2 == 0 else [zeros, qh], axis=0)

    def logits(ks, h):
        k2 = k_ref[pl.ds(ks, tq), (h // 2) * LANES:(h // 2 + 1) * LANES]
        return _dot(k2, q_slot(h))

    def step(j, carry, masked):
        ks = pl.multiple_of(j * tq, tq)
        new = [None] * N_HEADS

        def stage2(h, z, sp, cum):
            r, acc = carry[h]
            a = jnp.exp(z - sp - cum - r)
            if masked:
                a = jnp.where(strict, a, 0.0)
            pv = _dot(vt_ref[j, h * HEAD_DIM:(h + 1) * HEAD_DIM, :], a.astype(bf16))
            return h, r + jnp.sum(sp, axis=0, keepdims=True), acc, pv

        z_next = logits(ks, 0)
        mid = None
        tail = None
        for h in range(N_HEADS):
            z = z_next
            if h + 1 < N_HEADS:
                z_next = logits(ks, h + 1)
            sp = jnp.maximum(z, 0.0) + jnp.log(1.0 + jnp.exp(-jnp.abs(z)))
            if masked:
                sp = jnp.where(strict, sp, 0.0)
            hi, lo = _split_bf16(sp)
            cum = _dot(later, hi) + _dot(later, lo)
            if mid is not None:
                done = stage2(*mid)
                if tail is not None:
                    g, r_g, acc_g, pv_g = tail
                    new[g] = (r_g, acc_g + pv_g)
                tail = done
            mid = (h, z, sp, cum)
        done = stage2(*mid)
        for g, r_g, acc_g, pv_g in (tail, done):
            new[g] = (r_g, acc_g + pv_g)
        return tuple(new)

    def all_dead(carry):
        lowest = carry[0][0]
        for h in range(1, N_HEADS):
            lowest = jnp.minimum(lowest, carry[h][0])
        return jnp.min(lowest) > SB_DEAD

    init = tuple((jnp.zeros((1, tq), f32), jnp.zeros((HEAD_DIM, tq), f32)) for _ in range(N_HEADS))
    carry = step(qi, init, True)

    def cond(state):
        i, dead, _ = state
        return jnp.logical_and(i < qi, jnp.logical_not(dead))

    def body(state):
        i, _, carry = state
        carry = step(qi - 1 - i, carry, False)
        return i + 1, all_dead(carry), carry

    _, _, carry = lax.while_loop(cond, body, (jnp.int32(0), all_dead(carry), carry))
    _store_heads(o_ref, [acc for (_, acc) in carry])


def _prompt_attn_call(kern, qt, k, vt, *, batch, seq, tq, name):
    nq = seq // tq
    return pl.pallas_call(
        functools.partial(kern, tq=tq),
        out_shape=jax.ShapeDtypeStruct((batch * seq, N_HEADS * HEAD_DIM), jnp.float32),
        grid=(batch, nq),
        in_specs=[pl.BlockSpec((None, qt.shape[1], tq), lambda b, i: (b * nq + i, 0, 0)),
                  pl.BlockSpec((seq, k.shape[1]), lambda b, i: (b, 0)),
                  pl.BlockSpec((nq, vt.shape[1], tq), lambda b, i: (b, 0, 0))],
        out_specs=pl.BlockSpec((tq, N_HEADS * HEAD_DIM), lambda b, i: (b * nq + i, 0)),
        compiler_params=_cparams(("parallel", "arbitrary")),
        name=name,
    )(qt, k, vt)


N_QH = 64


def _softplus(z):
    return jnp.maximum(z, 0.0) + jnp.log1p(jnp.exp(-jnp.abs(z)))


def _split_bf16(x):
    hi = x.astype(jnp.bfloat16)
    return hi, (x - hi.astype(jnp.float32)).astype(jnp.bfloat16)


def _own_head(full):
    rowi = lax.broadcasted_iota(jnp.int32, full.shape, 0)
    lanei = lax.broadcasted_iota(jnp.int32, full.shape, 1)
    kept = jnp.where(lanei // HEAD_DIM == rowi % N_HEADS, full, 0.0)
    return jnp.sum(kept.reshape(N_QH // N_HEADS, N_HEADS, full.shape[1]), axis=1)


def _pad_rows(x, rows):
    return jnp.concatenate([x, jnp.zeros((rows - x.shape[0], x.shape[1]), x.dtype)], axis=0)


SB_SUB = 2


MLA_SLOTS = 4
MLA_AHEAD = MLA_SLOTS - 1


def _sample_attn_kernel(pt_ref, qabs_ref, qrope_ref, sq_ref, cn_ref, krn_ref, skn_ref, svn_ref, wuv_ref,
                        ckv_hbm, kr_hbm, sbk_hbm, sbv_hbm, o_mla_ref, o_sb_ref,
                        r_sc, accsb_sc, qbd_sc, kbuf, vbuf, sems, ckbuf, krbuf, ckb_sc, msems, *, n_pg, n_pages):
    seq = pl.program_id(0)
    n_seq = pl.num_programs(0)
    n_sub = n_pages // SB_SUB
    n_chunks = n_pages // n_pg
    bf16 = jnp.bfloat16

    def mla_copies(g):
        s_idx, chunk, slot = g // n_chunks, g % n_chunks, g % MLA_SLOTS
        out = []
        for p in range(n_pg):
            page = pt_ref[s_idx, n_pages - 1 - (chunk * n_pg + p)]
            out.append(pltpu.make_async_copy(ckv_hbm.at[page], ckbuf.at[slot, p], msems.at[slot]))
            out.append(pltpu.make_async_copy(kr_hbm.at[page], krbuf.at[slot, :, pl.ds(p * PAGE, PAGE)],
                                             msems.at[slot]))
        return out

    def sb_copies(s_idx, sub, slot):
        out = []
        for p in range(SB_SUB):
            page = pt_ref[s_idx, n_pages - 1 - (sub * SB_SUB + p)]
            out.append(pltpu.make_async_copy(sbk_hbm.at[page], kbuf.at[slot, p], sems.at[slot]))
            out.append(pltpu.make_async_copy(sbv_hbm.at[page], vbuf.at[slot, p], sems.at[slot]))
        return out
    qabs = qabs_ref[...]
    qrope = qrope_ref[...]

    later = (lax.broadcasted_iota(jnp.int32, (PAGE, PAGE), 0)
             > lax.broadcasted_iota(jnp.int32, (PAGE, PAGE), 1)).astype(bf16)

    def suffix_in_block(lg):
        hi, lo = _split_bf16(lg)
        both = _dot(jnp.concatenate([hi, lo], axis=0), later)
        return both[:N_QH] + both[N_QH:]

    g0 = seq * n_chunks
    n_total = n_seq * n_chunks

    @pl.when(seq == 0)
    def _():
        for g in range(MLA_AHEAD):
            for cp in mla_copies(g):
                cp.start()

    def new_rows_and_stick_breaking():
        sq16 = _pad_rows(sq_ref[...], 16).astype(bf16)
        pick = (lax.broadcasted_iota(jnp.int32, (N_QH, 16), 0) // N_HEADS
                == lax.broadcasted_iota(jnp.int32, (N_QH, 16), 1)).astype(bf16)
        rep = _dot(pick, sq16)
        rowi = lax.broadcasted_iota(jnp.int32, rep.shape, 0)
        lanei = lax.broadcasted_iota(jnp.int32, rep.shape, 1)
        qbd = jnp.where(lanei // HEAD_DIM == rowi % N_HEADS, rep, 0.0).astype(bf16)
        qbd_sc[...] = qbd

        key = lax.broadcasted_iota(jnp.int32, (N_QH, PAGE), 1)
        qry = lax.broadcasted_iota(jnp.int32, (N_QH, PAGE), 0) // N_HEADS
        cn = _pad_rows(cn_ref[...], PAGE).astype(bf16)
        krn = _pad_rows(krn_ref[...], PAGE).astype(bf16)
        s = _dot_nt(qabs, cn) + _dot_nt(qrope, krn)
        s = jnp.where(key <= qry, s, NEG_BIG)
        m = jnp.max(s, axis=1, keepdims=True)
        p = jnp.exp2(s - m)
        l = jnp.sum(p, axis=1, keepdims=True)
        acc = _dot(p.astype(bf16), cn)

        strict = key < qry
        z = _dot_nt(qbd, _pad_rows(skn_ref[...], PAGE).astype(bf16))
        lg = jnp.where(strict, -_softplus(z), 0.0)
        a = jnp.where(strict, jnp.exp(z + lg + suffix_in_block(lg)), 0.0)
        accsb_sc[...] = _dot(a.astype(bf16), _pad_rows(svn_ref[...], PAGE).astype(bf16))
        r_sc[...] = jnp.sum(lg, axis=1, keepdims=True)

        @pl.when(seq == 0)
        def _():
            for cp in sb_copies(0, 0, 0):
                cp.start()

        def dead_now():
            return jnp.max(r_sc[...]) < -SB_DEAD

        def sb_round(state):
            i, _ = state
            slot = i % 2
            for cp in sb_copies(seq, i, slot):
                cp.wait()

            @pl.when(i + 1 < n_sub)
            def _():
                for cp in sb_copies(seq, i + 1, 1 - slot):
                    cp.start()

            r = r_sc[...]
            sb = jnp.zeros((N_QH, SB_WIDTH), jnp.float32)
            for p in range(SB_SUB):
                z = _dot(qbd, kbuf[slot, p].astype(bf16))
                lg = -_softplus(z)
                a = jnp.exp(z + lg + suffix_in_block(lg) + r)
                sb = sb + _dot_nt(a.astype(bf16), vbuf[slot, p].astype(bf16))
                r = r + jnp.sum(lg, axis=1, keepdims=True)
            r_sc[...] = r
            accsb_sc[...] = accsb_sc[...] + sb
            return i + 1, dead_now()

        n_done, _ = lax.while_loop(lambda st: jnp.logical_and(st[0] < n_sub, jnp.logical_not(st[1])),
                                   sb_round, (jnp.int32(0), dead_now()))

        @pl.when(n_done < n_sub)
        def _():
            for cp in sb_copies(seq, n_done, n_done % 2):
                cp.wait()

        @pl.when(seq + 1 < n_seq)
        def _():
            for cp in sb_copies(seq + 1, 0, 0):
                cp.start()

        return m, l, acc

    m, l, acc = new_rows_and_stick_breaking()

    def scores(c):
        slot = (g0 + c) % MLA_SLOTS
        for cp in mla_copies(g0 + c):
            cp.wait()
        ckb_sc[c % 2] = ckbuf[slot].reshape(n_pg * PAGE, KV_LORA).astype(bf16)
        return _dot_nt(qabs, ckb_sc[c % 2]) + _dot(qrope, krbuf[slot].astype(bf16))

    s_next = scores(0)
    for c in range(n_chunks):
        if c + MLA_AHEAD < n_chunks:
            for cp in mla_copies(g0 + c + MLA_AHEAD):
                cp.start()
        else:
            @pl.when(g0 + c + MLA_AHEAD < n_total)
            def _(c=c):
                for cp in mla_copies(g0 + c + MLA_AHEAD):
                    cp.start()
        s = s_next
        if c + 1 < n_chunks:
            s_next = scores(c + 1)
        m_new = jnp.maximum(m, jnp.max(s, axis=1, keepdims=True))
        alpha = jnp.exp2(m - m_new)
        e = jnp.exp2(s - m_new)
        l = alpha * l + jnp.sum(e, axis=1, keepdims=True)
        acc = acc * alpha + _dot(e.astype(bf16), ckb_sc[c % 2])
        m = m_new

    o_mla_ref[...] = _own_head(_dot((acc / l).astype(bf16), wuv_ref[...]))
    o_sb_ref[...] = _own_head(accsb_sc[...])


def _feature_major(cache):
    return jnp.moveaxis(cache, 1, -1).reshape(cache.shape[0], -1, PAGE)


def _sample_attn_call(page_table, qabs, qrope, sq, ckv_n, kr_n, sk_n, sv_n, w_uvf, cache_ckv, cache_krope,
                      cache_sb_k, cache_sb_v, *, n_pg):
    n_seq, n_pages = page_table.shape
    n_chunks = n_pages // n_pg
    t_new = ckv_n.shape[0] // n_seq

    def seq_blk(shape):
        return pl.BlockSpec((None,) + shape, lambda s, pt: (s, 0, 0))

    def new_blk(w):
        return pl.BlockSpec((t_new, w), lambda s, pt: (s, 0))

    hbm = pl.BlockSpec(memory_space=pl.ANY)
    in_specs = [seq_blk((N_QH, KV_LORA)), seq_blk((N_QH, ROPE)), new_blk(SB_WIDTH), new_blk(KV_LORA),
                new_blk(ROPE), new_blk(SB_WIDTH), new_blk(SB_WIDTH),
                pl.BlockSpec(w_uvf.shape, lambda s, pt: (0, 0)), hbm, hbm, hbm, hbm]
    args = [qabs.reshape(n_seq, N_QH, KV_LORA), qrope.reshape(n_seq, N_QH, ROPE), sq, ckv_n, kr_n, sk_n, sv_n, w_uvf,
            cache_ckv, cache_krope, cache_sb_k, cache_sb_v]
    out_blk = pl.BlockSpec((t_new, MLA_WIDTH), lambda s, pt: (s, 0))
    f32, bf16 = jnp.float32, jnp.bfloat16
    assert n_pages % n_pg == 0 and n_pages % SB_SUB == 0 and n_seq * n_chunks >= MLA_AHEAD
    return pl.pallas_call(
        functools.partial(_sample_attn_kernel, n_pg=n_pg, n_pages=n_pages),
        out_shape=[jax.ShapeDtypeStruct((n_seq * t_new, MLA_WIDTH), f32),
                   jax.ShapeDtypeStruct((n_seq * t_new, SB_WIDTH), f32)],
        grid_spec=pltpu.PrefetchScalarGridSpec(
            num_scalar_prefetch=1, grid=(n_seq,), in_specs=in_specs, out_specs=[out_blk, out_blk],
            scratch_shapes=[pltpu.VMEM((N_QH, 1), f32), pltpu.VMEM((N_QH, SB_WIDTH), f32),
                            pltpu.VMEM((N_QH, SB_WIDTH), bf16),
                            pltpu.VMEM((2, SB_SUB, SB_WIDTH, PAGE), f32), pltpu.VMEM((2, SB_SUB, SB_WIDTH, PAGE), f32),
                            pltpu.SemaphoreType.DMA((2,)),
                            pltpu.VMEM((MLA_SLOTS, n_pg, PAGE, KV_LORA), f32),
                            pltpu.VMEM((MLA_SLOTS, ROPE, n_pg * PAGE), f32),
                            pltpu.VMEM((2, n_pg * PAGE, KV_LORA), bf16),
                            pltpu.SemaphoreType.DMA((MLA_SLOTS,))]),
        compiler_params=_cparams(("arbitrary",)),
        name="sample_attn",
    )(page_table, *args)


def _lane_pack(cols):
    lane = lax.broadcasted_iota(jnp.int32, (cols[0].shape[0], LANES), 1)
    out = jnp.zeros((cols[0].shape[0], LANES), jnp.float32)
    for k, col in enumerate(cols):
        out = jnp.where(lane == k, col, out)
    return out


def _out_router_kernel(omla_ref, osb_ref, x_ref, gm_ref, gs_ref, wout_ref, gmoe_ref, wr_hi_ref, wr_lo_ref, br_ref,
                       h_ref, tok_ref, eidx_ref, gate_ref, rank_ref, cnt_ref, carry_sc, *, tm):
    bf16 = jnp.bfloat16

    @pl.when(pl.program_id(0) == 0)
    def _():
        carry_sc[...] = jnp.zeros_like(carry_sc)

    o = jnp.concatenate([_rms(omla_ref[...], gm_ref[...]), _rms(osb_ref[...], gs_ref[...])], axis=1).astype(bf16)
    h = x_ref[...] + _dot(o, wout_ref[...])
    h_ref[...] = h
    tok = _rms(h, gmoe_ref[...])
    tok_ref[...] = tok
    t_hi, t_lo = _split_bf16(tok)
    wr_hi = wr_hi_ref[...]
    logits = _dot(t_hi, wr_hi) + (_dot(t_hi, wr_lo_ref[...]) + _dot(t_lo, wr_hi)) + br_ref[...]

    lane = lax.broadcasted_iota(jnp.int32, (tm, LANES), 1).astype(jnp.float32)
    work = logits
    vals, idxs, sels = [], [], []
    for _ in range(TOP_K):
        v = jnp.max(work, axis=1, keepdims=True)
        idx = jnp.min(jnp.where(work == v, lane, float(LANES)), axis=1, keepdims=True)
        sel = lane == idx
        work = jnp.where(sel, -jnp.inf, work)
        vals.append(v)
        idxs.append(idx)
        sels.append(sel)
    exps = [jnp.exp(v - vals[0]) for v in vals]
    denom = exps[0] + exps[1] + exps[2] + exps[3]
    gates = [e / denom for e in exps]

    assign = jnp.zeros((tm, LANES), jnp.float32)
    for sel in sels:
        assign = jnp.where(sel, 1.0, assign)
    earlier = (lax.broadcasted_iota(jnp.int32, (tm, tm), 1)
               < lax.broadcasted_iota(jnp.int32, (tm, tm), 0)).astype(bf16)
    prefix = _dot(earlier, assign.astype(bf16)) + carry_sc[...]
    ranks = [jnp.sum(jnp.where(sel, prefix, 0.0), axis=1, keepdims=True) for sel in sels]
    carry = carry_sc[...] + jnp.sum(assign, axis=0, keepdims=True)
    carry_sc[...] = carry
    cnt_ref[...] = carry
    eidx_ref[...] = _lane_pack(idxs)
    gate_ref[...] = _lane_pack(gates)
    rank_ref[...] = _lane_pack(ranks)


def _out_router_call(o_mla, o_sb, x2d, wts, *, tm):
    n = x2d.shape[0]
    tok = lambda w: pl.BlockSpec((tm, w), lambda i: (i, 0))
    full = lambda a: pl.BlockSpec(a.shape, lambda i: (0,) * a.ndim)
    f32 = jnp.float32
    sds = jax.ShapeDtypeStruct
    ws = [wts['g_mla_out'], wts['g_sb_out'], wts['w_out'], wts['g_moe'], wts['wr_hi'], wts['wr_lo'], wts['b_router']]
    return pl.pallas_call(
        functools.partial(_out_router_kernel, tm=tm),
        out_shape=[sds((n, D_MODEL), f32), sds((n, D_MODEL), f32), sds((n, LANES), f32), sds((n, LANES), f32),
                   sds((n, LANES), f32), sds((1, LANES), f32)],
        grid=(n // tm,),
        in_specs=[tok(MLA_WIDTH), tok(SB_WIDTH), tok(D_MODEL)] + [full(a) for a in ws],
        out_specs=[tok(D_MODEL), tok(D_MODEL), tok(LANES), tok(LANES), tok(LANES),
                   pl.BlockSpec((1, LANES), lambda i: (0, 0))],
        scratch_shapes=[pltpu.VMEM((1, LANES), f32)],
        compiler_params=_cparams(("arbitrary",)),
        name="out_router",
    )(o_mla, o_sb, x2d, *ws)


def _row_copy(src, s, dst, d, sem):
    return pltpu.make_async_copy(src.at[pl.ds(s, 1), :], dst.at[pl.ds(d, 1), :], sem)


def _dispatch_kernel(dest_ref, tokp_ref, toks_ref, xs_ref, sem, *, tm, n_prompt_tiles):
    i = pl.program_id(0)

    def scatter(tok_ref):
        def start(t, _):
            for k in range(TOP_K):
                _row_copy(tok_ref, t, xs_ref, dest_ref[0, t * TOP_K + k], sem).start()
            return 0

        def wait(t, _):
            for k in range(TOP_K):
                _row_copy(tok_ref, 0, xs_ref, 0, sem).wait()
            return 0

        lax.fori_loop(0, tm, start, 0)
        lax.fori_loop(0, tm, wait, 0)

    @pl.when(i < n_prompt_tiles)
    def _():
        scatter(tokp_ref)

    @pl.when(i >= n_prompt_tiles)
    def _():
        scatter(toks_ref)


def _dispatch_call(dest_tiles, tok_p, tok_s, *, tm):
    npt, nst = tok_p.shape[0] // tm, tok_s.shape[0] // tm
    n_pairs = (tok_p.shape[0] + tok_s.shape[0]) * TOP_K
    return pl.pallas_call(
        functools.partial(_dispatch_kernel, tm=tm, n_prompt_tiles=npt),
        out_shape=jax.ShapeDtypeStruct((n_pairs, D_MODEL), jnp.float32),
        grid=(npt + nst,),
        in_specs=[pl.BlockSpec((None, 1, tm * TOP_K), lambda i: (i, 0, 0), memory_space=pltpu.SMEM),
                  pl.BlockSpec((tm, D_MODEL), lambda i: (jnp.minimum(i, npt - 1), 0)),
                  pl.BlockSpec((tm, D_MODEL), lambda i: (jnp.maximum(i - npt, 0), 0))],
        out_specs=pl.BlockSpec(memory_space=pl.ANY),
        scratch_shapes=[pltpu.SemaphoreType.DMA],
        compiler_params=_cparams(("arbitrary",)),
        name="moe_dispatch",
    )(dest_tiles, tok_p, tok_s)


def _expert_kernel(blk_ref, exp_ref, lo_ref, hi_ref, first_ref, xs_ref, wup_ref, bup_ref, wdn_ref, bdn_ref,
                   y_ref, wup_sc, wdn_sc, *, bm):
    i = pl.program_id(0)
    bf16 = jnp.bfloat16
    lo, hi = lo_ref[i], hi_ref[i]
    e_now = exp_ref[i]
    e_prev = exp_ref[jnp.maximum(i - 1, 0)]

    @pl.when((i == 0) | (e_now != e_prev))
    def _():
        wup_sc[...] = wup_ref[...].astype(bf16)
        wdn_sc[...] = wdn_ref[...].astype(bf16)

    @pl.when(hi > lo)
    def _():
        x = xs_ref[...].astype(bf16)
        hcat = _dot(x, wup_sc[...]) + bup_ref[...]
        x_glu = jnp.minimum(hcat[:, :D_FF], SWIGLU_LIMIT)
        x_lin = jnp.clip(hcat[:, D_FF:], -SWIGLU_LIMIT, SWIGLU_LIMIT)
        act = x_glu * jax.nn.sigmoid(SWIGLU_ALPHA * x_glu) * (x_lin + 1.0)
        y = _dot(act.astype(bf16), wdn_sc[...]) + bdn_ref[...]
        rows = lax.broadcasted_iota(jnp.int32, (bm, 1), 0)
        y = jnp.where((rows >= lo) & (rows < hi), y, 0.0)

        @pl.when(first_ref[i] == 1)
        def _():
            y_ref[...] = y

        @pl.when(first_ref[i] == 0)
        def _():
            y_ref[...] = y_ref[...] + y


def _expert_call(items, xs, w_up, b_up, w_dn, b_dn, *, bm):
    n_items = items[0].shape[0]
    return pl.pallas_call(
        functools.partial(_expert_kernel, bm=bm),
        out_shape=jax.ShapeDtypeStruct(xs.shape, jnp.float32),
        grid_spec=pltpu.PrefetchScalarGridSpec(
            num_scalar_prefetch=5, grid=(n_items,),
            in_specs=[pl.BlockSpec((bm, D_MODEL), lambda i, blk, ex, lo, hi, fi: (blk[i], 0)),
                      pl.BlockSpec((None, D_MODEL, 2 * D_FF), lambda i, blk, ex, lo, hi, fi: (ex[i], 0, 0)),
                      pl.BlockSpec((None, 1, 2 * D_FF), lambda i, blk, ex, lo, hi, fi: (ex[i], 0, 0)),
                      pl.BlockSpec((None, D_FF, D_MODEL), lambda i, blk, ex, lo, hi, fi: (ex[i], 0, 0)),
                      pl.BlockSpec((None, 1, D_MODEL), lambda i, blk, ex, lo, hi, fi: (ex[i], 0, 0))],
            out_specs=pl.BlockSpec((bm, D_MODEL), lambda i, blk, ex, lo, hi, fi: (blk[i], 0)),
            scratch_shapes=[pltpu.VMEM((D_MODEL, 2 * D_FF), jnp.bfloat16), pltpu.VMEM((D_FF, D_MODEL), jnp.bfloat16)]),
        compiler_params=_cparams(("arbitrary",)),
        name="moe_experts",
    )(*items, xs, w_up, b_up.reshape(N_EXPERTS, 1, 2 * D_FF), w_dn, b_dn.reshape(N_EXPERTS, 1, D_MODEL))


def _combine_kernel(dest_ref, dest_next_ref, h_ref, gate_ref, gf_ref, ys_ref, out_ref, buf, sems, *, tm):
    i = pl.program_id(0)
    slot = i % 2

    def gather(d_ref, sl):
        def start(t, _):
            for k in range(TOP_K):
                pltpu.make_async_copy(ys_ref.at[pl.ds(d_ref[0, t * TOP_K + k], 1), :],
                                      buf.at[sl, k, pl.ds(t, 1), :], sems.at[sl]).start()
            return 0
        lax.fori_loop(0, tm, start, 0)

    @pl.when(i == 0)
    def _():
        gather(dest_ref, 0)

    @pl.when(i + 1 < pl.num_programs(0))
    def _():
        gather(dest_next_ref, 1 - slot)

    def wait(t, _):
        for k in range(TOP_K):
            pltpu.make_async_copy(ys_ref.at[pl.ds(0, 1), :], buf.at[slot, 0, pl.ds(0, 1), :], sems.at[slot]).wait()
        return 0

    lax.fori_loop(0, tm, wait, 0)
    gate = gate_ref[...]
    h = h_ref[...]
    for k in range(TOP_K):
        h = h + gate[:, k:k + 1] * buf[slot, k]
    out_ref[...] = _rms(h, gf_ref[...])


def _combine_call(dest_tiles, h, gate, g_final, ys, *, tm):
    n = h.shape[0]
    nt = n // tm
    dest_blk = lambda f: pl.BlockSpec((None, 1, tm * TOP_K), lambda i: (f(i), 0, 0), memory_space=pltpu.SMEM)
    return pl.pallas_call(
        functools.partial(_combine_kernel, tm=tm),
        out_shape=jax.ShapeDtypeStruct((n, D_MODEL), jnp.float32),
        grid=(nt,),
        in_specs=[dest_blk(lambda i: i), dest_blk(lambda i: jnp.minimum(i + 1, nt - 1)),
                  pl.BlockSpec((tm, D_MODEL), lambda i: (i, 0)),
                  pl.BlockSpec((tm, LANES), lambda i: (i, 0)),
                  pl.BlockSpec((1, D_MODEL), lambda i: (0, 0)),
                  pl.BlockSpec(memory_space=pl.ANY)],
        out_specs=pl.BlockSpec((tm, D_MODEL), lambda i: (i, 0)),
        scratch_shapes=[pltpu.VMEM((2, TOP_K, tm, D_MODEL), jnp.float32), pltpu.SemaphoreType.DMA((2,))],
        compiler_params=_cparams(("arbitrary",)),
        name="moe_combine",
    )(dest_tiles, dest_tiles, h, gate, g_final.reshape(1, D_MODEL), ys)


def _plan_items(counts, n_pairs, bm):
    n_blocks = n_pairs // bm
    n_items = n_blocks + N_EXPERTS - 1
    experts = jnp.arange(N_EXPERTS, dtype=jnp.int32)
    ends = jnp.cumsum(counts)
    starts = ends - counts
    first_blk = starts // bm
    last_blk = jnp.where(counts > 0, (ends - 1) // bm, first_blk - 1)
    n_touch = jnp.maximum(last_blk - first_blk + 1, 0)
    item_end = jnp.cumsum(n_touch)
    item_start = item_end - n_touch
    it = jnp.arange(n_items, dtype=jnp.int32)
    valid = it < item_end[-1]
    last_e = jnp.max(jnp.where(counts > 0, experts, 0))
    e_of = jnp.where(valid, jnp.sum(it[:, None] >= item_end[None, :], axis=1), last_e).astype(jnp.int32)
    mine = e_of[:, None] == experts[None, :]
    pick = lambda v: jnp.sum(jnp.where(mine, v[None, :], 0), axis=1)
    blk_of = jnp.where(valid, pick(first_blk) + it - pick(item_start), n_blocks - 1).astype(jnp.int32)
    lo = jnp.where(valid, jnp.clip(pick(starts) - blk_of * bm, 0, bm), 0).astype(jnp.int32)
    hi = jnp.where(valid, jnp.clip(pick(ends) - blk_of * bm, 0, bm), 0).astype(jnp.int32)
    prev_blk = jnp.concatenate([jnp.full((1,), -1, jnp.int32), blk_of[:-1]])
    first = (valid & (blk_of != prev_blk)).astype(jnp.int32)
    return blk_of, e_of.astype(jnp.int32), lo, hi, first


def kernel(x_prompt, x_sample, cache_ckv, cache_krope, cache_sb_k, cache_sb_v, page_table, g_attn_norm, w_in,
           g_q_norm, w_q_up, g_kv_norm, w_kv_up, g_mla_out, g_sb_out, w_out, g_moe_norm, w_router, b_router,
           w_moe_up, b_moe_up, w_moe_down, b_moe_down, g_final):
    b_p, s_p, d = x_prompt.shape
    b_s, s_s, _ = x_sample.shape
    n_p, n_s = b_p * s_p, b_s * s_s
    n_pool = cache_ckv.shape[1]
    past_len = page_table.shape[1] * PAGE
    bf16 = jnp.bfloat16
    wts = _prep_weights(g_attn_norm[0], w_in[0], g_q_norm[0], w_q_up[0], g_kv_norm[0], w_kv_up[0])
    w_r = jnp.pad(w_router[0], ((0, 0), (0, LANES - N_EXPERTS)))
    wr_hi = w_r.astype(bf16)
    wts.update(
        g_mla_out=g_mla_out[0].reshape(1, MLA_WIDTH), g_sb_out=g_sb_out[0].reshape(1, SB_WIDTH),
        w_out=w_out[0].astype(bf16), g_moe=g_moe_norm[0].reshape(1, D_MODEL), wr_hi=wr_hi,
        wr_lo=(w_r - wr_hi.astype(jnp.float32)).astype(bf16),
        b_router=jnp.pad(b_router[0], (0, LANES - N_EXPERTS), constant_values=NEG_BIG).reshape(1, LANES))
    x_p2 = x_prompt.reshape(n_p, d)
    x_s2 = x_sample.reshape(n_s, d)
    tm = 256

    cos_p, sin_p = _rope_tables(jnp.arange(s_p))
    (ckv_p, krt_p, skt_p, svt_p, qt_p, k_p, vt_p, sqt_p, skb_p, svtb_p) = _proj_call(
        x_p2, cos_p, sin_p, wts, sample=False, seq_tiles=s_p // tm, tm=tm)
    o_mla_p = _prompt_attn_call(_mla_prompt_kernel, qt_p, k_p, vt_p, batch=b_p, seq=s_p, tq=tm, name="mla_prompt")
    o_sb_p = _prompt_attn_call(_sb_prompt_kernel, sqt_p, skb_p, svtb_p, batch=b_p, seq=s_p, tq=tm, name="sb_prompt")
    h_p, tok_p, eidx_p, gate_p, rank_p, cnt_p = _out_router_call(o_mla_p, o_sb_p, x_p2, wts, tm=tm)

    cos_s, sin_s = _rope_tables(past_len + jnp.arange(s_s))
    cos_s, sin_s = jnp.tile(cos_s, (b_s, 1)), jnp.tile(sin_s, (b_s, 1))
    (ckv_s, kr_s, sk_s, sv_s, sq_s, qabs_s, qrope_s) = _proj_call(
        x_s2, cos_s, sin_s, wts, sample=True, seq_tiles=n_s // tm, tm=tm)
    o_mla_s, o_sb_s = _sample_attn_call(
        page_table, qabs_s, qrope_s, sq_s.astype(jnp.float32), ckv_s, kr_s, sk_s, sv_s, wts['w_uvf'],
        cache_ckv[0], _feature_major(cache_krope[0]), _feature_major(cache_sb_k[0]), _feature_major(cache_sb_v[0]),
        n_pg=8)
    h_s, tok_s, eidx_s, gate_s, rank_s, cnt_s = _out_router_call(o_mla_s, o_sb_s, x_s2, wts, tm=tm)

    cnt_p = cnt_p[0, :N_EXPERTS].astype(jnp.int32)
    cnt_s = cnt_s[0, :N_EXPERTS].astype(jnp.int32)
    counts = cnt_p + cnt_s
    pstart = jnp.cumsum(counts) - counts
    e_p = eidx_p[:, :TOP_K].astype(jnp.int32)
    e_s = eidx_s[:, :TOP_K].astype(jnp.int32)
    experts = jnp.arange(N_EXPERTS, dtype=jnp.int32)
    lookup = lambda table, e: jnp.sum(jnp.where(e[..., None] == experts, table, 0), axis=-1)
    dest_p = lookup(pstart, e_p) + rank_p[:, :TOP_K].astype(jnp.int32)
    dest_s = lookup(pstart + cnt_p, e_s) + rank_s[:, :TOP_K].astype(jnp.int32)
    dest_p = dest_p.reshape(n_p // tm, 1, tm * TOP_K)
    dest_s = dest_s.reshape(n_s // tm, 1, tm * TOP_K)
    bm = 512
    items = _plan_items(counts, (n_p + n_s) * TOP_K, bm)

    xs = _dispatch_call(jnp.concatenate([dest_p, dest_s], axis=0), tok_p, tok_s, tm=tm)
    ys = _expert_call(items, xs, w_moe_up[0], b_moe_up[0], w_moe_down[0], b_moe_down[0], bm=bm)
    y_p = _combine_call(dest_p, h_p, gate_p, g_final, ys, tm=tm)
    y_s = _combine_call(dest_s, h_s, gate_s, g_final, ys, tm=tm)

    return (y_p.reshape(b_p, s_p, d), y_s.reshape(b_s, s_s, d),
            ckv_p.reshape(1, b_p, s_p, KV_LORA), jnp.swapaxes(krt_p, 1, 2)[None],
            jnp.moveaxis(skt_p.reshape(b_p, N_HEADS, HEAD_DIM, s_p), 3, 1)[None],
            jnp.moveaxis(svt_p.reshape(b_p, N_HEADS, HEAD_DIM, s_p), 3, 1)[None],
            ckv_s.reshape(1, b_s, s_s, KV_LORA), kr_s.reshape(1, b_s, s_s, ROPE),
            sk_s.reshape(1, b_s, s_s, N_HEADS, HEAD_DIM), sv_s.reshape(1, b_s, s_s, N_HEADS, HEAD_DIM))
```

```python
import functools
import math

import jax
import jax.numpy as jnp
from jax import lax
from jax.experimental import pallas as pl
from jax.experimental.pallas import tpu as pltpu

D_MODEL = 1024
HEAD_DIM = 64
N_HEADS = 8
NOPE = 64
ROPE = 32
HALF_ROPE = ROPE // 2
Q_LORA = 384
KV_LORA = 256
SB_WIDTH = N_HEADS * HEAD_DIM
MLA_WIDTH = N_HEADS * HEAD_DIM
HEAD_PAD = 128
QK_WIDTH = N_HEADS * HEAD_PAD
MLA_SCALE = 1.0 / math.sqrt(NOPE + ROPE)
SB_SCALE = 1.0 / math.sqrt(HEAD_DIM)
ROPE_THETA = 10000.0
N_EXPERTS = 32
TOP_K = 4
D_FF = 1024
SWIGLU_ALPHA = 1.702
SWIGLU_LIMIT = 7.0
PAGE = 128
EPS = 1e-6
LANES = 128
NEG_BIG = -1e30
LOG2E = 1.4426950408889634
SB_DEAD = 104.0

_O_CQ = 0
_O_CKV = _O_CQ + Q_LORA
_O_SQ = _O_CKV + KV_LORA
_O_SK = _O_SQ + SB_WIDTH
_O_SV = _O_SK + SB_WIDTH
_O_KR = _O_SV + SB_WIDTH
IN_PAD = _O_KR + HEAD_PAD

VMEM_LIMIT = 56 * 1024 * 1024


def _cparams(sem, vmem=VMEM_LIMIT):
    return pltpu.CompilerParams(dimension_semantics=sem, vmem_limit_bytes=vmem)


def _rms(x, g):
    return x * lax.rsqrt(jnp.mean(x * x, axis=-1, keepdims=True) + EPS) * g


def _dot(a, b):
    return jnp.dot(a, b, preferred_element_type=jnp.float32)


def _dot_nt(a, b):
    return lax.dot_general(a, b, (((1,), (1,)), ((), ())), preferred_element_type=jnp.float32)


def _dot_tn(a, b):
    return lax.dot_general(a, b, (((0,), (0,)), ((), ())), preferred_element_type=jnp.float32)


def _rope_slot(t, cos_t, sin_t):
    lane = lax.broadcasted_iota(jnp.int32, t.shape, 1)
    partner = jnp.where(lane < NOPE + HALF_ROPE,
                        pltpu.roll(t, LANES - HALF_ROPE, 1),
                        pltpu.roll(t, HALF_ROPE, 1))
    return t * cos_t + partner * sin_t


def _proj_kernel(x_ref, cos_ref, sin_ref, g_attn_ref, w_in_ref, g_q_ref, w_q_ref, g_kv_ref,
                 w_ukp_ref, w_uvt_ref, *rest, sample):
    bf16 = jnp.bfloat16
    if sample:
        (w_ukt_ref, sel_ref, ckv_ref, kr_ref, sk_ref, sv_ref, sq_ref, qabs_ref, qrope_ref) = rest
    else:
        (ckv_ref, krt_ref, skt_ref, svt_ref, qt_ref, k_ref, vt_ref, sqt_ref, skb_ref, svtb_ref) = rest
    x = x_ref[...]
    xb = _rms(x, g_attn_ref[...]).astype(bf16)
    proj = _dot(xb, w_in_ref[...])
    cos_t = cos_ref[...]
    sin_t = sin_ref[...]

    cqn = _rms(proj[:, _O_CQ:_O_CKV], g_q_ref[...]).astype(bf16)
    q = _dot(cqn, w_q_ref[...])
    c_kv = _rms(proj[:, _O_CKV:_O_SQ], g_kv_ref[...])
    ckv_ref[...] = c_kv
    ckb = c_kv.astype(bf16)
    kr = _rope_slot(proj[:, _O_KR:IN_PAD], cos_t, sin_t)
    sq = proj[:, _O_SQ:_O_SK] * SB_SCALE
    sk = proj[:, _O_SK:_O_SV]
    sv = proj[:, _O_SV:_O_KR]
    q_heads = [_rope_slot(q[:, h * HEAD_PAD:(h + 1) * HEAD_PAD], cos_t, sin_t) * (MLA_SCALE * LOG2E)
               for h in range(N_HEADS)]

    if sample:
        kr_ref[...] = kr[:, NOPE:NOPE + ROPE]
        sk_ref[...] = sk
        sv_ref[...] = sv
        sq_ref[...] = sq.astype(bf16)
        for h in range(N_HEADS):
            qabs_ref[:, h * KV_LORA:(h + 1) * KV_LORA] = _dot(q_heads[h].astype(bf16), w_ukt_ref[h]).astype(bf16)
        qcat = jnp.concatenate([qh.astype(bf16) for qh in q_heads], axis=1)
        qrope_ref[...] = _dot(qcat, sel_ref[...]).astype(bf16)
    else:
        krt_ref[...] = kr.T[NOPE:NOPE + ROPE, :]
        sk_t = sk.T
        sv_t = sv.T
        skt_ref[...] = sk_t
        svt_ref[...] = sv_t
        svtb_ref[...] = sv_t.astype(bf16)
        skb_ref[...] = sk.astype(bf16)
        sqt_ref[...] = sq.T.astype(bf16)
        k_nope = _dot(ckb, w_ukp_ref[...])
        for h in range(N_HEADS):
            sl = slice(h * HEAD_PAD, (h + 1) * HEAD_PAD)
            qt_ref[sl, :] = q_heads[h].T.astype(bf16)
            k_ref[:, sl] = (k_nope[:, sl] + kr).astype(bf16)
        vt_ref[...] = _dot_nt(w_uvt_ref[...], ckb).astype(bf16)


def _proj_call(x2d, cos_t, sin_t, wts, *, sample, seq_tiles, tm):
    n = x2d.shape[0]
    nt = n // tm
    tok = lambda w: pl.BlockSpec((tm, w), lambda i: (i, 0))
    tab = pl.BlockSpec((tm, LANES), lambda i: (i % seq_tiles, 0))
    full = lambda a: pl.BlockSpec(a.shape, lambda i: (0,) * a.ndim)
    f32, bf16 = jnp.float32, jnp.bfloat16
    sds = jax.ShapeDtypeStruct
    ins = [x2d, cos_t, sin_t, wts['g_attn'], wts['w_in'], wts['g_q'], wts['w_q'], wts['g_kv'],
           wts['w_ukp'], wts['w_uvt']]
    in_specs = [tok(D_MODEL), tab, tab] + [full(a) for a in ins[3:]]
    if sample:
        ins += [wts['w_ukt'], wts['sel']]
        in_specs += [full(wts['w_ukt']), full(wts['sel'])]
        outs = [sds((n, KV_LORA), f32), sds((n, ROPE), f32), sds((n, SB_WIDTH), f32), sds((n, SB_WIDTH), f32),
                sds((n, SB_WIDTH), bf16), sds((n, N_HEADS * KV_LORA), bf16), sds((n, N_HEADS * ROPE), bf16)]
        out_specs = [tok(KV_LORA), tok(ROPE), tok(SB_WIDTH), tok(SB_WIDTH), tok(SB_WIDTH),
                     tok(N_HEADS * KV_LORA), tok(N_HEADS * ROPE)]
    else:
        n_b = nt // seq_tiles
        seq = seq_tiles * tm
        fm_leaf = lambda w: pl.BlockSpec((None, w, tm), lambda i: (i // seq_tiles, 0, i % seq_tiles))
        fm_tile = lambda w: pl.BlockSpec((None, w, tm), lambda i: (i, 0, 0))
        outs = [sds((n, KV_LORA), f32), sds((n_b, ROPE, seq), f32), sds((n_b, SB_WIDTH, seq), f32),
                sds((n_b, SB_WIDTH, seq), f32), sds((nt, QK_WIDTH, tm), bf16), sds((n, QK_WIDTH), bf16),
                sds((nt, MLA_WIDTH, tm), bf16), sds((nt, SB_WIDTH, tm), bf16), sds((n, SB_WIDTH), bf16),
                sds((nt, SB_WIDTH, tm), bf16)]
        out_specs = [tok(KV_LORA), fm_leaf(ROPE), fm_leaf(SB_WIDTH), fm_leaf(SB_WIDTH), fm_tile(QK_WIDTH),
                     tok(QK_WIDTH), fm_tile(MLA_WIDTH), fm_tile(SB_WIDTH), tok(SB_WIDTH), fm_tile(SB_WIDTH)]
    return pl.pallas_call(
        functools.partial(_proj_kernel, sample=sample),
        out_shape=outs, grid=(nt,), in_specs=in_specs, out_specs=out_specs,
        compiler_params=_cparams(("parallel",)),
        name="proj_sample" if sample else "proj_prompt",
    )(*ins)


def _rope_tables(pos):
    inv = ROPE_THETA ** (-jnp.arange(HALF_ROPE, dtype=jnp.float32) / HALF_ROPE)
    ang = pos.astype(jnp.float32)[:, None] * inv[None, :]
    cos, sin = jnp.cos(ang), jnp.sin(ang)
    t = pos.shape[0]
    ones, zeros = jnp.ones((t, NOPE), jnp.float32), jnp.zeros((t, NOPE), jnp.float32)
    pad = jnp.zeros((t, HEAD_PAD - NOPE - ROPE), jnp.float32)
    return (jnp.concatenate([ones, cos, cos, pad], axis=1),
            jnp.concatenate([zeros, -sin, sin, pad], axis=1))


def _prep_weights(g_attn_norm, w_in, g_q_norm, w_q_up, g_kv_norm, w_kv_up):
    bf16 = jnp.bfloat16
    o1, o2, o3 = Q_LORA, Q_LORA + KV_LORA, Q_LORA + KV_LORA + ROPE
    o4, o5 = o3 + SB_WIDTH, o3 + 2 * SB_WIDTH
    kr_slot = jnp.concatenate([jnp.zeros((D_MODEL, NOPE), w_in.dtype), w_in[:, o2:o3],
                               jnp.zeros((D_MODEL, HEAD_PAD - NOPE - ROPE), w_in.dtype)], axis=1)
    w_in_r = jnp.concatenate([w_in[:, :o2], w_in[:, o3:o4], w_in[:, o4:o5], w_in[:, o5:], kr_slot], axis=1)
    w_q_slot = jnp.concatenate([w_q_up, jnp.zeros((Q_LORA, N_HEADS, HEAD_PAD - NOPE - ROPE), w_q_up.dtype)],
                               axis=2).reshape(Q_LORA, QK_WIDTH)
    w_uk = w_kv_up[:, :, :NOPE]
    w_uv = w_kv_up[:, :, NOPE:]
    w_ukp = jnp.concatenate([w_uk, jnp.zeros_like(w_uk)], axis=2).reshape(KV_LORA, QK_WIDTH)
    w_ukt = jnp.concatenate([jnp.transpose(w_uk, (1, 2, 0)),
                             jnp.zeros((N_HEADS, HEAD_PAD - NOPE, KV_LORA), w_uk.dtype)], axis=1)
    r = jnp.arange(QK_WIDTH)
    c = jnp.arange(N_HEADS * ROPE)
    sel = ((r[:, None] // HEAD_PAD == c[None, :] // ROPE)
           & (r[:, None] % HEAD_PAD == NOPE + c[None, :] % ROPE)).astype(bf16)
    return dict(
        g_attn=g_attn_norm.reshape(1, D_MODEL), w_in=w_in_r.astype(bf16), g_q=g_q_norm.reshape(1, Q_LORA),
        w_q=w_q_slot.astype(bf16), g_kv=g_kv_norm.reshape(1, KV_LORA), w_ukp=w_ukp.astype(bf16),
        w_uvf=w_uv.reshape(KV_LORA, MLA_WIDTH).astype(bf16),
        w_uvt=w_uv.reshape(KV_LORA, MLA_WIDTH).T.astype(bf16), w_ukt=w_ukt.astype(bf16), sel=sel)


def _store_heads(o_ref, outs_t):
    for pair in range(N_HEADS // 2):
        both = jnp.concatenate([outs_t[2 * pair], outs_t[2 * pair + 1]], axis=0)
        o_ref[:, pair * LANES:(pair + 1) * LANES] = both.T


def _mla_prompt_kernel(qt_ref, k_ref, vt_ref, o_ref, *, tq):
    qi = pl.program_id(1)
    f32, bf16 = jnp.float32, jnp.bfloat16
    key = lax.broadcasted_iota(jnp.int32, (tq, tq), 0)
    qry = lax.broadcasted_iota(jnp.int32, (tq, tq), 1)
    causal = key <= qry

    def scores(ks, h):
        k = k_ref[pl.ds(ks, tq), h * HEAD_PAD:(h + 1) * HEAD_PAD]
        return _dot(k, qt_ref[h * HEAD_PAD:(h + 1) * HEAD_PAD, :])

    def step(j, carry, masked):
        ks = pl.multiple_of(j * tq, tq)
        new = [None] * N_HEADS
        ahead = [scores(ks, 0), scores(ks, 1)]
        pending = None
        for h in range(N_HEADS):
            m, l, acc = carry[h]
            s = ahead.pop(0)
            if h + 2 < N_HEADS:
                ahead.append(scores(ks, h + 2))
            if masked:
                s = jnp.where(causal, s, NEG_BIG)
            m_new = jnp.maximum(m, jnp.max(s, axis=0, keepdims=True))
            p = jnp.exp2(s - m_new)
            alpha = jnp.exp2(m - m_new)
            l = alpha * l + jnp.sum(p, axis=0, keepdims=True)
            pv = _dot(vt_ref[j, h * HEAD_DIM:(h + 1) * HEAD_DIM, :], p.astype(bf16))
            if pending is not None:
                g, m_g, l_g, alpha_g, acc_g, pv_g = pending
                new[g] = (m_g, l_g, alpha_g * acc_g + pv_g)
            pending = (h, m_new, l, alpha, acc, pv)
        g, m_g, l_g, alpha_g, acc_g, pv_g = pending
        new[g] = (m_g, l_g, alpha_g * acc_g + pv_g)
        return tuple(new)

    init = tuple((jnp.full((1, tq), NEG_BIG, f32), jnp.zeros((1, tq), f32), jnp.zeros((HEAD_DIM, tq), f32))
                 for _ in range(N_HEADS))
    def two_steps(i, carry):
        return step(2 * i + 1, step(2 * i, carry, False), False)

    carry = lax.fori_loop(0, qi // 2, two_steps, init)
    carry = lax.cond(qi % 2 == 1, lambda c: step(qi - 1, c, False), lambda c: c, carry)
    carry = step(qi, carry, True)
    _store_heads(o_ref, [acc / l for (_, l, acc) in carry])


def _sb_prompt_kernel(sqt_ref, k_ref, vt_ref, o_ref, *, tq):
    qi = pl.program_id(1)
    f32, bf16 = jnp.float32, jnp.bfloat16
    key = lax.broadcasted_iota(jnp.int32, (tq, tq), 0)
    qry = lax.broadcasted_iota(jnp.int32, (tq, tq), 1)
    strict = key < qry
    later = (qry > key).astype(bf16)
    zeros = jnp.zeros((HEAD_DIM, tq), bf16)

    def q_slot(h):
        qh = sqt_ref[h * HEAD_DIM:(h + 1) * HEAD_DIM, :]
        return jnp.concatenate([qh, zeros] if h % 2 == 0 else [zeros, qh], axis=0)

    def logits(ks, h):
        k2 = k_ref[pl.ds(ks, tq), (h // 2) * LANES:(h // 2 + 1) * LANES]
        return _dot(k2, q_slot(h))

    def step(j, carry, masked):
        ks = pl.multiple_of(j * tq, tq)
        new = [None] * N_HEADS

        def stage2(h, z, sp, cum):
            r, acc = carry[h]
            a = jnp.exp(z - sp - cum - r)
            if masked:
                a = jnp.where(strict, a, 0.0)
            pv = _dot(vt_ref[j, h * HEAD_DIM:(h + 1) * HEAD_DIM, :], a.astype(bf16))
            return h, r + jnp.sum(sp, axis=0, keepdims=True), acc, pv

        z_next = logits(ks, 0)
        mid = None
        tail = None
        for h in range(N_HEADS):
            z = z_next
            if h + 1 < N_HEADS:
                z_next = logits(ks, h + 1)
            sp = jnp.maximum(z, 0.0) + jnp.log(1.0 + jnp.exp(-jnp.abs(z)))
            if masked:
                sp = jnp.where(strict, sp, 0.0)
            hi, lo = _split_bf16(sp)
            cum = _dot(later, hi) + _dot(later, lo)
            if mid is not None:
                done = stage2(*mid)
                if tail is not None:
                    g, r_g, acc_g, pv_g = tail
                    new[g] = (r_g, acc_g + pv_g)
                tail = done
            mid = (h, z, sp, cum)
        done = stage2(*mid)
        for g, r_g, acc_g, pv_g in (tail, done):
            new[g] = (r_g, acc_g + pv_g)
        return tuple(new)

    def all_dead(carry):
        lowest = carry[0][0]
        for h in range(1, N_HEADS):
            lowest = jnp.minimum(lowest, carry[h][0])
        return jnp.min(lowest) > SB_DEAD

    init = tuple((jnp.zeros((1, tq), f32), jnp.zeros((HEAD_DIM, tq), f32)) for _ in range(N_HEADS))
    carry = step(qi, init, True)

    def cond(state):
        i, dead, _ = state
        return jnp.logical_and(i < qi, jnp.logical_not(dead))

    def body(state):
        i, _, carry = state
        carry = step(qi - 1 - i, carry, False)
        return i + 1, all_dead(carry), carry

    _, _, carry = lax.while_loop(cond, body, (jnp.int32(0), all_dead(carry), carry))
    _store_heads(o_ref, [acc for (_, acc) in carry])


def _prompt_attn_call(kern, qt, k, vt, *, batch, seq, tq, name):
    nq = seq // tq
    return pl.pallas_call(
        functools.partial(kern, tq=tq),
        out_shape=jax.ShapeDtypeStruct((batch * seq, N_HEADS * HEAD_DIM), jnp.float32),
        grid=(batch, nq),
        in_specs=[pl.BlockSpec((None, qt.shape[1], tq), lambda b, i: (b * nq + i, 0, 0)),
                  pl.BlockSpec((seq, k.shape[1]), lambda b, i: (b, 0)),
                  pl.BlockSpec((nq, vt.shape[1], tq), lambda b, i: (b, 0, 0))],
        out_specs=pl.BlockSpec((tq, N_HEADS * HEAD_DIM), lambda b, i: (b * nq + i, 0)),
        compiler_params=_cparams(("parallel", "arbitrary")),
        name=name,
    )(qt, k, vt)


N_QH = 64


def _softplus(z):
    return jnp.maximum(z, 0.0) + jnp.log1p(jnp.exp(-jnp.abs(z)))


def _split_bf16(x):
    hi = x.astype(jnp.bfloat16)
    return hi, (x - hi.astype(jnp.float32)).astype(jnp.bfloat16)


def _own_head(full):
    rowi = lax.broadcasted_iota(jnp.int32, full.shape, 0)
    lanei = lax.broadcasted_iota(jnp.int32, full.shape, 1)
    kept = jnp.where(lanei // HEAD_DIM == rowi % N_HEADS, full, 0.0)
    return jnp.sum(kept.reshape(N_QH // N_HEADS, N_HEADS, full.shape[1]), axis=1)


def _pad_rows(x, rows):
    return jnp.concatenate([x, jnp.zeros((rows - x.shape[0], x.shape[1]), x.dtype)], axis=0)


SB_SUB = 2


MLA_SLOTS = 4
MLA_AHEAD = MLA_SLOTS - 1


def _sample_attn_kernel(pt_ref, qabs_ref, qrope_ref, sq_ref, cn_ref, krn_ref, skn_ref, svn_ref, wuv_ref,
                        ckv_hbm, kr_hbm, sbk_hbm, sbv_hbm, o_mla_ref, o_sb_ref,
                        r_sc, accsb_sc, qbd_sc, kbuf, vbuf, sems, ckbuf, krbuf, ckb_sc, msems, *, n_pg, n_pages):
    seq = pl.program_id(0)
    n_seq = pl.num_programs(0)
    n_sub = n_pages // SB_SUB
    n_chunks = n_pages // n_pg
    bf16 = jnp.bfloat16

    def mla_copies(g):
        s_idx, chunk, slot = g // n_chunks, g % n_chunks, g % MLA_SLOTS
        out = []
        for p in range(n_pg):
            page = pt_ref[s_idx, n_pages - 1 - (chunk * n_pg + p)]
            out.append(pltpu.make_async_copy(ckv_hbm.at[page], ckbuf.at[slot, p], msems.at[slot]))
            out.append(pltpu.make_async_copy(kr_hbm.at[page], krbuf.at[slot, :, pl.ds(p * PAGE, PAGE)],
                                             msems.at[slot]))
        return out

    def sb_copies(s_idx, sub, slot):
        out = []
        for p in range(SB_SUB):
            page = pt_ref[s_idx, n_pages - 1 - (sub * SB_SUB + p)]
            out.append(pltpu.make_async_copy(sbk_hbm.at[page], kbuf.at[slot, p], sems.at[slot]))
            out.append(pltpu.make_async_copy(sbv_hbm.at[page], vbuf.at[slot, p], sems.at[slot]))
        return out
    qabs = qabs_ref[...]
    qrope = qrope_ref[...]

    later = (lax.broadcasted_iota(jnp.int32, (PAGE, PAGE), 0)
             > lax.broadcasted_iota(jnp.int32, (PAGE, PAGE), 1)).astype(bf16)

    def suffix_in_block(lg):
        hi, lo = _split_bf16(lg)
        both = _dot(jnp.concatenate([hi, lo], axis=0), later)
        return both[:N_QH] + both[N_QH:]

    g0 = seq * n_chunks
    n_total = n_seq * n_chunks

    @pl.when(seq == 0)
    def _():
        for g in range(MLA_AHEAD):
            for cp in mla_copies(g):
                cp.start()

    def new_rows_and_stick_breaking():
        sq16 = _pad_rows(sq_ref[...], 16).astype(bf16)
        pick = (lax.broadcasted_iota(jnp.int32, (N_QH, 16), 0) // N_HEADS
                == lax.broadcasted_iota(jnp.int32, (N_QH, 16), 1)).astype(bf16)
        rep = _dot(pick, sq16)
        rowi = lax.broadcasted_iota(jnp.int32, rep.shape, 0)
        lanei = lax.broadcasted_iota(jnp.int32, rep.shape, 1)
        qbd = jnp.where(lanei // HEAD_DIM == rowi % N_HEADS, rep, 0.0).astype(bf16)
        qbd_sc[...] = qbd

        key = lax.broadcasted_iota(jnp.int32, (N_QH, PAGE), 1)
        qry = lax.broadcasted_iota(jnp.int32, (N_QH, PAGE), 0) // N_HEADS
        cn = _pad_rows(cn_ref[...], PAGE).astype(bf16)
        krn = _pad_rows(krn_ref[...], PAGE).astype(bf16)
        s = _dot_nt(qabs, cn) + _dot_nt(qrope, krn)
        s = jnp.where(key <= qry, s, NEG_BIG)
        m = jnp.max(s, axis=1, keepdims=True)
        p = jnp.exp2(s - m)
        l = jnp.sum(p, axis=1, keepdims=True)
        acc = _dot(p.astype(bf16), cn)

        strict = key < qry
        z = _dot_nt(qbd, _pad_rows(skn_ref[...], PAGE).astype(bf16))
        lg = jnp.where(strict, -_softplus(z), 0.0)
        a = jnp.where(strict, jnp.exp(z + lg + suffix_in_block(lg)), 0.0)
        accsb_sc[...] = _dot(a.astype(bf16), _pad_rows(svn_ref[...], PAGE).astype(bf16))
        r_sc[...] = jnp.sum(lg, axis=1, keepdims=True)

        @pl.when(seq == 0)
        def _():
            for cp in sb_copies(0, 0, 0):
                cp.start()

        def dead_now():
            return jnp.max(r_sc[...]) < -SB_DEAD

        def sb_round(state):
            i, _ = state
            slot = i % 2
            for cp in sb_copies(seq, i, slot):
                cp.wait()

            @pl.when(i + 1 < n_sub)
            def _():
                for cp in sb_copies(seq, i + 1, 1 - slot):
                    cp.start()

            r = r_sc[...]
            sb = jnp.zeros((N_QH, SB_WIDTH), jnp.float32)
            for p in range(SB_SUB):
                z = _dot(qbd, kbuf[slot, p].astype(bf16))
                lg = -_softplus(z)
                a = jnp.exp(z + lg + suffix_in_block(lg) + r)
                sb = sb + _dot_nt(a.astype(bf16), vbuf[slot, p].astype(bf16))
                r = r + jnp.sum(lg, axis=1, keepdims=True)
            r_sc[...] = r
            accsb_sc[...] = accsb_sc[...] + sb
            return i + 1, dead_now()

        n_done, _ = lax.while_loop(lambda st: jnp.logical_and(st[0] < n_sub, jnp.logical_not(st[1])),
                                   sb_round, (jnp.int32(0), dead_now()))

        @pl.when(n_done < n_sub)
        def _():
            for cp in sb_copies(seq, n_done, n_done % 2):
                cp.wait()

        @pl.when(seq + 1 < n_seq)
        def _():
            for cp in sb_copies(seq + 1, 0, 0):
                cp.start()

        return m, l, acc

    m, l, acc = new_rows_and_stick_breaking()

    def scores(c):
        slot = (g0 + c) % MLA_SLOTS
        for cp in mla_copies(g0 + c):
            cp.wait()
        ckb_sc[c % 2] = ckbuf[slot].reshape(n_pg * PAGE, KV_LORA).astype(bf16)
        part = n_pg * PAGE // 4
        latent = jnp.concatenate([_dot_nt(qabs, ckb_sc[c % 2, q * part:(q + 1) * part]) for q in range(4)], axis=1)
        return latent + _dot(qrope, krbuf[slot].astype(bf16))

    s_next = scores(0)
    for c in range(n_chunks):
        if c + MLA_AHEAD < n_chunks:
            for cp in mla_copies(g0 + c + MLA_AHEAD):
                cp.start()
        else:
            @pl.when(g0 + c + MLA_AHEAD < n_total)
            def _(c=c):
                for cp in mla_copies(g0 + c + MLA_AHEAD):
                    cp.start()
        s = s_next
        if c + 1 < n_chunks:
            s_next = scores(c + 1)
        m_new = jnp.maximum(m, jnp.max(s, axis=1, keepdims=True))
        alpha = jnp.exp2(m - m_new)
        e = jnp.exp2(s - m_new)
        l = alpha * l + jnp.sum(e, axis=1, keepdims=True)
        acc = acc * alpha + _dot(e.astype(bf16), ckb_sc[c % 2])
        m = m_new

    o_mla_ref[...] = _own_head(_dot((acc / l).astype(bf16), wuv_ref[...]))
    o_sb_ref[...] = _own_head(accsb_sc[...])


def _feature_major(cache):
    return jnp.moveaxis(cache, 1, -1).reshape(cache.shape[0], -1, PAGE)


def _sample_attn_call(page_table, qabs, qrope, sq, ckv_n, kr_n, sk_n, sv_n, w_uvf, cache_ckv, cache_krope,
                      cache_sb_k, cache_sb_v, *, n_pg):
    n_seq, n_pages = page_table.shape
    n_chunks = n_pages // n_pg
    t_new = ckv_n.shape[0] // n_seq

    def seq_blk(shape):
        return pl.BlockSpec((None,) + shape, lambda s, pt: (s, 0, 0))

    def new_blk(w):
        return pl.BlockSpec((t_new, w), lambda s, pt: (s, 0))

    hbm = pl.BlockSpec(memory_space=pl.ANY)
    in_specs = [seq_blk((N_QH, KV_LORA)), seq_blk((N_QH, ROPE)), new_blk(SB_WIDTH), new_blk(KV_LORA),
                new_blk(ROPE), new_blk(SB_WIDTH), new_blk(SB_WIDTH),
                pl.BlockSpec(w_uvf.shape, lambda s, pt: (0, 0)), hbm, hbm, hbm, hbm]
    args = [qabs.reshape(n_seq, N_QH, KV_LORA), qrope.reshape(n_seq, N_QH, ROPE), sq, ckv_n, kr_n, sk_n, sv_n, w_uvf,
            cache_ckv, cache_krope, cache_sb_k, cache_sb_v]
    out_blk = pl.BlockSpec((t_new, MLA_WIDTH), lambda s, pt: (s, 0))
    f32, bf16 = jnp.float32, jnp.bfloat16
    assert n_pages % n_pg == 0 and n_pages % SB_SUB == 0 and n_seq * n_chunks >= MLA_AHEAD
    return pl.pallas_call(
        functools.partial(_sample_attn_kernel, n_pg=n_pg, n_pages=n_pages),
        out_shape=[jax.ShapeDtypeStruct((n_seq * t_new, MLA_WIDTH), f32),
                   jax.ShapeDtypeStruct((n_seq * t_new, SB_WIDTH), f32)],
        grid_spec=pltpu.PrefetchScalarGridSpec(
            num_scalar_prefetch=1, grid=(n_seq,), in_specs=in_specs, out_specs=[out_blk, out_blk],
            scratch_shapes=[pltpu.VMEM((N_QH, 1), f32), pltpu.VMEM((N_QH, SB_WIDTH), f32),
                            pltpu.VMEM((N_QH, SB_WIDTH), bf16),
                            pltpu.VMEM((2, SB_SUB, SB_WIDTH, PAGE), f32), pltpu.VMEM((2, SB_SUB, SB_WIDTH, PAGE), f32),
                            pltpu.SemaphoreType.DMA((2,)),
                            pltpu.VMEM((MLA_SLOTS, n_pg, PAGE, KV_LORA), f32),
                            pltpu.VMEM((MLA_SLOTS, ROPE, n_pg * PAGE), f32),
                            pltpu.VMEM((2, n_pg * PAGE, KV_LORA), bf16),
                            pltpu.SemaphoreType.DMA((MLA_SLOTS,))]),
        compiler_params=_cparams(("arbitrary",)),
        name="sample_attn",
    )(page_table, *args)


def _out_router_kernel(omla_ref, osb_ref, x_ref, gm_ref, gs_ref, wout_ref, gmoe_ref, wr_hi_ref, wr_lo_ref, br_ref,
                       h_ref, tok_ref, eidx_ref, gate_ref, rank_ref, cnt_ref, carry_sc, *, tm):
    bf16 = jnp.bfloat16

    @pl.when(pl.program_id(0) == 0)
    def _():
        carry_sc[...] = jnp.zeros_like(carry_sc)

    o = jnp.concatenate([_rms(omla_ref[...], gm_ref[...]), _rms(osb_ref[...], gs_ref[...])], axis=1).astype(bf16)
    h = x_ref[...] + _dot(o, wout_ref[...])
    h_ref[...] = h
    tok = _rms(h, gmoe_ref[...])
    tok_ref[...] = tok
    t_hi, t_lo = _split_bf16(tok)
    wr_hi = wr_hi_ref[...]
    logits = _dot(t_hi, wr_hi) + (_dot(t_hi, wr_lo_ref[...]) + _dot(t_lo, wr_hi))
    work = logits.T[:N_EXPERTS] + br_ref[...]
    expert = lax.broadcasted_iota(jnp.int32, (N_EXPERTS, tm), 0).astype(jnp.float32)
    vals, idxs, sels = [], [], []
    for _ in range(TOP_K):
        v = jnp.max(work, axis=0, keepdims=True)
        idx = jnp.min(jnp.where(work == v, expert, float(N_EXPERTS)), axis=0, keepdims=True)
        sel = expert == idx
        work = jnp.where(sel, -jnp.inf, work)
        vals.append(v)
        idxs.append(idx)
        sels.append(sel)
    exps = [jnp.exp(v - vals[0]) for v in vals]
    denom = exps[0] + exps[1] + exps[2] + exps[3]
    gates = [e / denom for e in exps]

    assign = jnp.zeros((N_EXPERTS, tm), jnp.float32)
    for sel in sels:
        assign = jnp.where(sel, 1.0, assign)
    before = (lax.broadcasted_iota(jnp.int32, (tm, tm), 0)
              < lax.broadcasted_iota(jnp.int32, (tm, tm), 1)).astype(bf16)
    prefix = _dot(assign.astype(bf16), before) + carry_sc[...]
    ranks = [jnp.sum(jnp.where(sel, prefix, 0.0), axis=0, keepdims=True) for sel in sels]
    carry = carry_sc[...] + jnp.sum(assign, axis=1, keepdims=True)
    carry_sc[...] = carry
    cnt_ref[...] = carry
    pad = jnp.zeros((8 - TOP_K, tm), jnp.float32)
    eidx_ref[...] = jnp.concatenate(idxs + [pad], axis=0)
    gate_ref[...] = jnp.concatenate(gates + [pad], axis=0)
    rank_ref[...] = jnp.concatenate(ranks + [pad], axis=0)


def _router_weights(w_router, b_router):
    w_p = jnp.pad(w_router, ((0, 0), (0, LANES - N_EXPERTS)))
    hi = w_p.astype(jnp.bfloat16)
    return dict(wr_hi=hi, wr_lo=(w_p - hi.astype(jnp.float32)).astype(jnp.bfloat16),
                b_router=b_router.reshape(N_EXPERTS, 1))


def _destinations(row_start, eidx_t, rank_t):
    e = eidx_t[:TOP_K].T.astype(jnp.int32)
    experts = jnp.arange(N_EXPERTS, dtype=jnp.int32)
    start = jnp.sum(jnp.where(e[..., None] == experts, row_start, 0), axis=-1)
    return start + rank_t[:TOP_K].T.astype(jnp.int32)


def _token_major(gate_t):
    return jnp.pad(gate_t[:TOP_K].T, ((0, 0), (0, LANES - TOP_K)))


def _out_router_call(o_mla, o_sb, x2d, wts, *, tm):
    n = x2d.shape[0]
    tok = lambda w: pl.BlockSpec((tm, w), lambda i: (i, 0))
    full = lambda a: pl.BlockSpec(a.shape, lambda i: (0,) * a.ndim)
    f32 = jnp.float32
    sds = jax.ShapeDtypeStruct
    ws = [wts['g_mla_out'], wts['g_sb_out'], wts['w_out'], wts['g_moe'], wts['wr_hi'], wts['wr_lo'], wts['b_router']]
    return pl.pallas_call(
        functools.partial(_out_router_kernel, tm=tm),
        out_shape=[sds((n, D_MODEL), f32), sds((n, D_MODEL), f32), sds((8, n), f32), sds((8, n), f32),
                   sds((8, n), f32), sds((N_EXPERTS, 1), f32)],
        grid=(n // tm,),
        in_specs=[tok(MLA_WIDTH), tok(SB_WIDTH), tok(D_MODEL)] + [full(a) for a in ws],
        out_specs=[tok(D_MODEL), tok(D_MODEL)] + [pl.BlockSpec((8, tm), lambda i: (0, i))] * 3
        + [pl.BlockSpec((N_EXPERTS, 1), lambda i: (0, 0))],
        scratch_shapes=[pltpu.VMEM((N_EXPERTS, 1), f32)],
        compiler_params=_cparams(("arbitrary",)),
        name="out_router",
    )(o_mla, o_sb, x2d, *ws)


def _row_copy(src, s, dst, d, sem):
    return pltpu.make_async_copy(src.at[pl.ds(s, 1), :], dst.at[pl.ds(d, 1), :], sem)


def _dispatch_kernel(dest_ref, tokp_ref, toks_ref, xs_ref, sem, *, tm, n_prompt_tiles):
    i = pl.program_id(0)

    def scatter(tok_ref):
        def start(t, _):
            for k in range(TOP_K):
                _row_copy(tok_ref, t, xs_ref, dest_ref[0, t * TOP_K + k], sem).start()
            return 0

        def wait(t, _):
            for k in range(TOP_K):
                _row_copy(tok_ref, 0, xs_ref, 0, sem).wait()
            return 0

        lax.fori_loop(0, tm, start, 0)
        lax.fori_loop(0, tm, wait, 0)

    @pl.when(i < n_prompt_tiles)
    def _():
        scatter(tokp_ref)

    @pl.when(i >= n_prompt_tiles)
    def _():
        scatter(toks_ref)


def _dispatch_call(dest_tiles, tok_p, tok_s, *, tm):
    npt, nst = tok_p.shape[0] // tm, tok_s.shape[0] // tm
    n_pairs = (tok_p.shape[0] + tok_s.shape[0]) * TOP_K
    return pl.pallas_call(
        functools.partial(_dispatch_kernel, tm=tm, n_prompt_tiles=npt),
        out_shape=jax.ShapeDtypeStruct((n_pairs, D_MODEL), jnp.float32),
        grid=(npt + nst,),
        in_specs=[pl.BlockSpec((None, 1, tm * TOP_K), lambda i: (i, 0, 0), memory_space=pltpu.SMEM),
                  pl.BlockSpec((tm, D_MODEL), lambda i: (jnp.minimum(i, npt - 1), 0)),
                  pl.BlockSpec((tm, D_MODEL), lambda i: (jnp.maximum(i - npt, 0), 0))],
        out_specs=pl.BlockSpec(memory_space=pl.ANY),
        scratch_shapes=[pltpu.SemaphoreType.DMA],
        compiler_params=_cparams(("arbitrary",)),
        name="moe_dispatch",
    )(dest_tiles, tok_p, tok_s)


EXPERT_SUB = 256


def _expert_kernel(blk_ref, exp_ref, lo_ref, hi_ref, first_ref, xs_ref, wup_ref, bup_ref, wdn_ref, bdn_ref,
                   y_ref, wup_sc, wdn_sc, *, bm):
    i = pl.program_id(0)
    bf16 = jnp.bfloat16
    lo, hi = lo_ref[i], hi_ref[i]
    e_now = exp_ref[i]
    e_prev = exp_ref[jnp.maximum(i - 1, 0)]

    @pl.when((i == 0) | (e_now != e_prev))
    def _():
        wup_sc[...] = wup_ref[...].astype(bf16)
        wdn_sc[...] = wdn_ref[...].astype(bf16)

    first = first_ref[i] == 1
    for r0 in range(0, bm, EXPERT_SUB):
        rs = slice(r0, r0 + EXPERT_SUB)
        touched = (lo < r0 + EXPERT_SUB) & (hi > r0)

        @pl.when(touched)
        def _(r0=r0, rs=rs):
            x = xs_ref[rs, :].astype(bf16)
            hcat = _dot(x, wup_sc[...]) + bup_ref[...]
            x_glu = jnp.minimum(hcat[:, :D_FF], SWIGLU_LIMIT)
            x_lin = jnp.clip(hcat[:, D_FF:], -SWIGLU_LIMIT, SWIGLU_LIMIT)
            act = x_glu * jax.nn.sigmoid(SWIGLU_ALPHA * x_glu) * (x_lin + 1.0)
            y = _dot(act.astype(bf16), wdn_sc[...]) + bdn_ref[...]
            rows = r0 + lax.broadcasted_iota(jnp.int32, (EXPERT_SUB, 1), 0)
            y = jnp.where((rows >= lo) & (rows < hi), y, 0.0)

            @pl.when(first)
            def _():
                y_ref[rs, :] = y

            @pl.when(jnp.logical_not(first))
            def _():
                y_ref[rs, :] = y_ref[rs, :] + y

        @pl.when(first & jnp.logical_not(touched))
        def _(rs=rs):
            y_ref[rs, :] = jnp.zeros((EXPERT_SUB, D_MODEL), jnp.float32)


def _expert_call(items, xs, w_up, b_up, w_dn, b_dn, *, bm):
    n_items = items[0].shape[0]
    return pl.pallas_call(
        functools.partial(_expert_kernel, bm=bm),
        out_shape=jax.ShapeDtypeStruct(xs.shape, jnp.float32),
        grid_spec=pltpu.PrefetchScalarGridSpec(
            num_scalar_prefetch=5, grid=(n_items,),
            in_specs=[pl.BlockSpec((bm, D_MODEL), lambda i, blk, ex, lo, hi, fi: (blk[i], 0)),
                      pl.BlockSpec((None, D_MODEL, 2 * D_FF), lambda i, blk, ex, lo, hi, fi: (ex[i], 0, 0)),
                      pl.BlockSpec((None, 1, 2 * D_FF), lambda i, blk, ex, lo, hi, fi: (ex[i], 0, 0)),
                      pl.BlockSpec((None, D_FF, D_MODEL), lambda i, blk, ex, lo, hi, fi: (ex[i], 0, 0)),
                      pl.BlockSpec((None, 1, D_MODEL), lambda i, blk, ex, lo, hi, fi: (ex[i], 0, 0))],
            out_specs=pl.BlockSpec((bm, D_MODEL), lambda i, blk, ex, lo, hi, fi: (blk[i], 0)),
            scratch_shapes=[pltpu.VMEM((D_MODEL, 2 * D_FF), jnp.bfloat16), pltpu.VMEM((D_FF, D_MODEL), jnp.bfloat16)]),
        compiler_params=_cparams(("arbitrary",)),
        name="moe_experts",
    )(*items, xs, w_up, b_up.reshape(N_EXPERTS, 1, 2 * D_FF), w_dn, b_dn.reshape(N_EXPERTS, 1, D_MODEL))


def _combine_kernel(dest_ref, dest_next_ref, h_ref, gate_ref, gf_ref, ys_ref, out_ref, buf, sems, *, tm):
    i = pl.program_id(0)
    slot = i % 2

    def gather(d_ref, sl):
        def start(t, _):
            for k in range(TOP_K):
                pltpu.make_async_copy(ys_ref.at[pl.ds(d_ref[0, t * TOP_K + k], 1), :],
                                      buf.at[sl, k, pl.ds(t, 1), :], sems.at[sl]).start()
            return 0
        lax.fori_loop(0, tm, start, 0)

    @pl.when(i == 0)
    def _():
        gather(dest_ref, 0)

    @pl.when(i + 1 < pl.num_programs(0))
    def _():
        gather(dest_next_ref, 1 - slot)

    def wait(t, _):
        for k in range(TOP_K):
            pltpu.make_async_copy(ys_ref.at[pl.ds(0, 1), :], buf.at[slot, 0, pl.ds(0, 1), :], sems.at[slot]).wait()
        return 0

    lax.fori_loop(0, tm, wait, 0)
    gate = gate_ref[...]
    h = h_ref[...]
    for k in range(TOP_K):
        h = h + gate[:, k:k + 1] * buf[slot, k]
    out_ref[...] = _rms(h, gf_ref[...])


def _combine_call(dest_tiles, h, gate, g_final, ys, *, tm):
    n = h.shape[0]
    nt = n // tm
    dest_blk = lambda f: pl.BlockSpec((None, 1, tm * TOP_K), lambda i: (f(i), 0, 0), memory_space=pltpu.SMEM)
    return pl.pallas_call(
        functools.partial(_combine_kernel, tm=tm),
        out_shape=jax.ShapeDtypeStruct((n, D_MODEL), jnp.float32),
        grid=(nt,),
        in_specs=[dest_blk(lambda i: i), dest_blk(lambda i: jnp.minimum(i + 1, nt - 1)),
                  pl.BlockSpec((tm, D_MODEL), lambda i: (i, 0)),
                  pl.BlockSpec((tm, LANES), lambda i: (i, 0)),
                  pl.BlockSpec((1, D_MODEL), lambda i: (0, 0)),
                  pl.BlockSpec(memory_space=pl.ANY)],
        out_specs=pl.BlockSpec((tm, D_MODEL), lambda i: (i, 0)),
        scratch_shapes=[pltpu.VMEM((2, TOP_K, tm, D_MODEL), jnp.float32), pltpu.SemaphoreType.DMA((2,))],
        compiler_params=_cparams(("arbitrary",)),
        name="moe_combine",
    )(dest_tiles, dest_tiles, h, gate, g_final.reshape(1, D_MODEL), ys)


def _plan_items(counts, n_pairs, bm):
    n_blocks = n_pairs // bm
    n_items = n_blocks + N_EXPERTS - 1
    experts = jnp.arange(N_EXPERTS, dtype=jnp.int32)
    ends = jnp.cumsum(counts)
    starts = ends - counts
    first_blk = starts // bm
    last_blk = jnp.where(counts > 0, (ends - 1) // bm, first_blk - 1)
    n_touch = jnp.maximum(last_blk - first_blk + 1, 0)
    item_end = jnp.cumsum(n_touch)
    item_start = item_end - n_touch
    it = jnp.arange(n_items, dtype=jnp.int32)
    valid = it < item_end[-1]
    last_e = jnp.max(jnp.where(counts > 0, experts, 0))
    e_of = jnp.where(valid, jnp.sum(it[:, None] >= item_end[None, :], axis=1), last_e).astype(jnp.int32)
    mine = e_of[:, None] == experts[None, :]
    pick = lambda v: jnp.sum(jnp.where(mine, v[None, :], 0), axis=1)
    blk_of = jnp.where(valid, pick(first_blk) + it - pick(item_start), n_blocks - 1).astype(jnp.int32)
    lo = jnp.where(valid, jnp.clip(pick(starts) - blk_of * bm, 0, bm), 0).astype(jnp.int32)
    hi = jnp.where(valid, jnp.clip(pick(ends) - blk_of * bm, 0, bm), 0).astype(jnp.int32)
    prev_blk = jnp.concatenate([jnp.full((1,), -1, jnp.int32), blk_of[:-1]])
    first = (valid & (blk_of != prev_blk)).astype(jnp.int32)
    return blk_of, e_of.astype(jnp.int32), lo, hi, first


def kernel(x_prompt, x_sample, cache_ckv, cache_krope, cache_sb_k, cache_sb_v, page_table, g_attn_norm, w_in,
           g_q_norm, w_q_up, g_kv_norm, w_kv_up, g_mla_out, g_sb_out, w_out, g_moe_norm, w_router, b_router,
           w_moe_up, b_moe_up, w_moe_down, b_moe_down, g_final):
    b_p, s_p, d = x_prompt.shape
    b_s, s_s, _ = x_sample.shape
    n_p, n_s = b_p * s_p, b_s * s_s
    n_pool = cache_ckv.shape[1]
    past_len = page_table.shape[1] * PAGE
    bf16 = jnp.bfloat16
    wts = _prep_weights(g_attn_norm[0], w_in[0], g_q_norm[0], w_q_up[0], g_kv_norm[0], w_kv_up[0])
    wts.update(_router_weights(w_router[0], b_router[0]))
    wts.update(
        g_mla_out=g_mla_out[0].reshape(1, MLA_WIDTH), g_sb_out=g_sb_out[0].reshape(1, SB_WIDTH),
        w_out=w_out[0].astype(bf16), g_moe=g_moe_norm[0].reshape(1, D_MODEL))
    x_p2 = x_prompt.reshape(n_p, d)
    x_s2 = x_sample.reshape(n_s, d)
    tm = 256

    cos_p, sin_p = _rope_tables(jnp.arange(s_p))
    (ckv_p, krt_p, skt_p, svt_p, qt_p, k_p, vt_p, sqt_p, skb_p, svtb_p) = _proj_call(
        x_p2, cos_p, sin_p, wts, sample=False, seq_tiles=s_p // tm, tm=tm)
    o_mla_p = _prompt_attn_call(_mla_prompt_kernel, qt_p, k_p, vt_p, batch=b_p, seq=s_p, tq=tm, name="mla_prompt")
    o_sb_p = _prompt_attn_call(_sb_prompt_kernel, sqt_p, skb_p, svtb_p, batch=b_p, seq=s_p, tq=tm, name="sb_prompt")
    h_p, tok_p, eidx_p, gate_p, rank_p, cnt_p = _out_router_call(o_mla_p, o_sb_p, x_p2, wts, tm=tm)

    cos_s, sin_s = _rope_tables(past_len + jnp.arange(s_s))
    cos_s, sin_s = jnp.tile(cos_s, (b_s, 1)), jnp.tile(sin_s, (b_s, 1))
    (ckv_s, kr_s, sk_s, sv_s, sq_s, qabs_s, qrope_s) = _proj_call(
        x_s2, cos_s, sin_s, wts, sample=True, seq_tiles=n_s // tm, tm=tm)
    o_mla_s, o_sb_s = _sample_attn_call(
        page_table, qabs_s, qrope_s, sq_s.astype(jnp.float32), ckv_s, kr_s, sk_s, sv_s, wts['w_uvf'],
        cache_ckv[0], _feature_major(cache_krope[0]), _feature_major(cache_sb_k[0]), _feature_major(cache_sb_v[0]),
        n_pg=8)
    h_s, tok_s, eidx_s, gate_s, rank_s, cnt_s = _out_router_call(o_mla_s, o_sb_s, x_s2, wts, tm=tm)

    cnt_p = cnt_p[:, 0].astype(jnp.int32)
    cnt_s = cnt_s[:, 0].astype(jnp.int32)
    counts = cnt_p + cnt_s
    pstart = jnp.cumsum(counts) - counts
    dest_p = _destinations(pstart, eidx_p, rank_p).reshape(n_p // tm, 1, tm * TOP_K)
    dest_s = _destinations(pstart + cnt_p, eidx_s, rank_s).reshape(n_s // tm, 1, tm * TOP_K)
    gate_p, gate_s = _token_major(gate_p), _token_major(gate_s)
    bm = 512
    items = _plan_items(counts, (n_p + n_s) * TOP_K, bm)

    xs = _dispatch_call(jnp.concatenate([dest_p, dest_s], axis=0), tok_p, tok_s, tm=tm)
    ys = _expert_call(items, xs, w_moe_up[0], b_moe_up[0], w_moe_down[0], b_moe_down[0], bm=bm)
    y_p = _combine_call(dest_p, h_p, gate_p, g_final, ys, tm=tm)
    y_s = _combine_call(dest_s, h_s, gate_s, g_final, ys, tm=tm)

    return (y_p.reshape(b_p, s_p, d), y_s.reshape(b_s, s_s, d),
            ckv_p.reshape(1, b_p, s_p, KV_LORA), jnp.swapaxes(krt_p, 1, 2)[None],
            jnp.moveaxis(skt_p.reshape(b_p, N_HEADS, HEAD_DIM, s_p), 3, 1)[None],
            jnp.moveaxis(svt_p.reshape(b_p, N_HEADS, HEAD_DIM, s_p), 3, 1)[None],
            ckv_s.reshape(1, b_s, s_s, KV_LORA), kr_s.reshape(1, b_s, s_s, ROPE),
            sk_s.reshape(1, b_s, s_s, N_HEADS, HEAD_DIM), sv_s.reshape(1, b_s, s_s, N_HEADS, HEAD_DIM))
```

```python
import functools
import math

import jax
import jax.numpy as jnp
from jax import lax
from jax.experimental import pallas as pl
from jax.experimental.pallas import tpu as pltpu

D_MODEL = 1024
HEAD_DIM = 64
N_HEADS = 8
NOPE = 64
ROPE = 32
HALF_ROPE = ROPE // 2
Q_LORA = 384
KV_LORA = 256
SB_WIDTH = N_HEADS * HEAD_DIM
MLA_WIDTH = N_HEADS * HEAD_DIM
HEAD_PAD = 128
QK_WIDTH = N_HEADS * HEAD_PAD
MLA_SCALE = 1.0 / math.sqrt(NOPE + ROPE)
SB_SCALE = 1.0 / math.sqrt(HEAD_DIM)
ROPE_THETA = 10000.0
N_EXPERTS = 32
TOP_K = 4
D_FF = 1024
SWIGLU_ALPHA = 1.702
SWIGLU_LIMIT = 7.0
PAGE = 128
EPS = 1e-6
LANES = 128
NEG_BIG = -1e30
LOG2E = 1.4426950408889634
SB_DEAD = 104.0

_O_CQ = 0
_O_CKV = _O_CQ + Q_LORA
_O_SQ = _O_CKV + KV_LORA
_O_SK = _O_SQ + SB_WIDTH
_O_SV = _O_SK + SB_WIDTH
_O_KR = _O_SV + SB_WIDTH
IN_PAD = _O_KR + HEAD_PAD

VMEM_LIMIT = 56 * 1024 * 1024


def _cparams(sem, vmem=VMEM_LIMIT):
    return pltpu.CompilerParams(dimension_semantics=sem, vmem_limit_bytes=vmem)


def _rms(x, g):
    return x * lax.rsqrt(jnp.mean(x * x, axis=-1, keepdims=True) + EPS) * g


def _dot(a, b):
    return jnp.dot(a, b, preferred_element_type=jnp.float32)


def _dot_nt(a, b):
    return lax.dot_general(a, b, (((1,), (1,)), ((), ())), preferred_element_type=jnp.float32)


def _dot_tn(a, b):
    return lax.dot_general(a, b, (((0,), (0,)), ((), ())), preferred_element_type=jnp.float32)


def _rope_slot(t, cos_t, sin_t):
    lane = lax.broadcasted_iota(jnp.int32, t.shape, 1)
    partner = jnp.where(lane < NOPE + HALF_ROPE,
                        pltpu.roll(t, LANES - HALF_ROPE, 1),
                        pltpu.roll(t, HALF_ROPE, 1))
    return t * cos_t + partner * sin_t


def _proj_kernel(x_ref, cos_ref, sin_ref, g_attn_ref, w_in_ref, g_q_ref, w_q_ref, g_kv_ref,
                 w_ukp_ref, w_uvt_ref, *rest, sample):
    bf16 = jnp.bfloat16
    if sample:
        (w_ukt_ref, sel_ref, ckv_ref, kr_ref, sk_ref, sv_ref, sq_ref, qabs_ref, qrope_ref) = rest
    else:
        (ckv_ref, krt_ref, skt_ref, svt_ref, qt_ref, k_ref, vt_ref, sqt_ref, skb_ref, svtb_ref) = rest
    x = x_ref[...]
    xb = _rms(x, g_attn_ref[...]).astype(bf16)
    proj = _dot(xb, w_in_ref[...])
    cos_t = cos_ref[...]
    sin_t = sin_ref[...]

    cqn = _rms(proj[:, _O_CQ:_O_CKV], g_q_ref[...]).astype(bf16)
    q = _dot(cqn, w_q_ref[...])
    c_kv = _rms(proj[:, _O_CKV:_O_SQ], g_kv_ref[...])
    ckv_ref[...] = c_kv
    ckb = c_kv.astype(bf16)
    kr = _rope_slot(proj[:, _O_KR:IN_PAD], cos_t, sin_t)
    sq = proj[:, _O_SQ:_O_SK] * SB_SCALE
    sk = proj[:, _O_SK:_O_SV]
    sv = proj[:, _O_SV:_O_KR]
    q_heads = [_rope_slot(q[:, h * HEAD_PAD:(h + 1) * HEAD_PAD], cos_t, sin_t) * (MLA_SCALE * LOG2E)
               for h in range(N_HEADS)]

    if sample:
        kr_ref[...] = kr[:, NOPE:NOPE + ROPE]
        sk_ref[...] = sk
        sv_ref[...] = sv
        sq_ref[...] = sq.astype(bf16)
        for h in range(N_HEADS):
            qabs_ref[:, h * KV_LORA:(h + 1) * KV_LORA] = _dot(q_heads[h].astype(bf16), w_ukt_ref[h]).astype(bf16)
        qcat = jnp.concatenate([qh.astype(bf16) for qh in q_heads], axis=1)
        qrope_ref[...] = _dot(qcat, sel_ref[...]).astype(bf16)
    else:
        krt_ref[...] = kr.T[NOPE:NOPE + ROPE, :]
        sk_t = sk.T
        sv_t = sv.T
        skt_ref[...] = sk_t
        svt_ref[...] = sv_t
        svtb_ref[...] = sv_t.astype(bf16)
        skb_ref[...] = sk.astype(bf16)
        sqt_ref[...] = sq.T.astype(bf16)
        k_nope = _dot(ckb, w_ukp_ref[...])
        for h in range(N_HEADS):
            sl = slice(h * HEAD_PAD, (h + 1) * HEAD_PAD)
            qt_ref[sl, :] = q_heads[h].T.astype(bf16)
            k_ref[:, sl] = (k_nope[:, sl] + kr).astype(bf16)
        vt_ref[...] = _dot_nt(w_uvt_ref[...], ckb).astype(bf16)


def _proj_call(x2d, cos_t, sin_t, wts, *, sample, seq_tiles, tm):
    n = x2d.shape[0]
    nt = n // tm
    tok = lambda w: pl.BlockSpec((tm, w), lambda i: (i, 0))
    tab = pl.BlockSpec((tm, LANES), lambda i: (i % seq_tiles, 0))
    full = lambda a: pl.BlockSpec(a.shape, lambda i: (0,) * a.ndim)
    f32, bf16 = jnp.float32, jnp.bfloat16
    sds = jax.ShapeDtypeStruct
    ins = [x2d, cos_t, sin_t, wts['g_attn'], wts['w_in'], wts['g_q'], wts['w_q'], wts['g_kv'],
           wts['w_ukp'], wts['w_uvt']]
    in_specs = [tok(D_MODEL), tab, tab] + [full(a) for a in ins[3:]]
    if sample:
        ins += [wts['w_ukt'], wts['sel']]
        in_specs += [full(wts['w_ukt']), full(wts['sel'])]
        outs = [sds((n, KV_LORA), f32), sds((n, ROPE), f32), sds((n, SB_WIDTH), f32), sds((n, SB_WIDTH), f32),
                sds((n, SB_WIDTH), bf16), sds((n, N_HEADS * KV_LORA), bf16), sds((n, N_HEADS * ROPE), bf16)]
        out_specs = [tok(KV_LORA), tok(ROPE), tok(SB_WIDTH), tok(SB_WIDTH), tok(SB_WIDTH),
                     tok(N_HEADS * KV_LORA), tok(N_HEADS * ROPE)]
    else:
        n_b = nt // seq_tiles
        seq = seq_tiles * tm
        fm_leaf = lambda w: pl.BlockSpec((None, w, tm), lambda i: (i // seq_tiles, 0, i % seq_tiles))
        fm_tile = lambda w: pl.BlockSpec((None, w, tm), lambda i: (i, 0, 0))
        outs = [sds((n, KV_LORA), f32), sds((n_b, ROPE, seq), f32), sds((n_b, SB_WIDTH, seq), f32),
                sds((n_b, SB_WIDTH, seq), f32), sds((nt, QK_WIDTH, tm), bf16), sds((n, QK_WIDTH), bf16),
                sds((nt, MLA_WIDTH, tm), bf16), sds((nt, SB_WIDTH, tm), bf16), sds((n, SB_WIDTH), bf16),
                sds((nt, SB_WIDTH, tm), bf16)]
        out_specs = [tok(KV_LORA), fm_leaf(ROPE), fm_leaf(SB_WIDTH), fm_leaf(SB_WIDTH), fm_tile(QK_WIDTH),
                     tok(QK_WIDTH), fm_tile(MLA_WIDTH), fm_tile(SB_WIDTH), tok(SB_WIDTH), fm_tile(SB_WIDTH)]
    return pl.pallas_call(
        functools.partial(_proj_kernel, sample=sample),
        out_shape=outs, grid=(nt,), in_specs=in_specs, out_specs=out_specs,
        compiler_params=_cparams(("parallel",)),
        name="proj_sample" if sample else "proj_prompt",
    )(*ins)


def _rope_tables(pos):
    inv = ROPE_THETA ** (-jnp.arange(HALF_ROPE, dtype=jnp.float32) / HALF_ROPE)
    ang = pos.astype(jnp.float32)[:, None] * inv[None, :]
    cos, sin = jnp.cos(ang), jnp.sin(ang)
    t = pos.shape[0]
    ones, zeros = jnp.ones((t, NOPE), jnp.float32), jnp.zeros((t, NOPE), jnp.float32)
    pad = jnp.zeros((t, HEAD_PAD - NOPE - ROPE), jnp.float32)
    return (jnp.concatenate([ones, cos, cos, pad], axis=1),
            jnp.concatenate([zeros, -sin, sin, pad], axis=1))


def _prep_weights(g_attn_norm, w_in, g_q_norm, w_q_up, g_kv_norm, w_kv_up):
    bf16 = jnp.bfloat16
    o1, o2, o3 = Q_LORA, Q_LORA + KV_LORA, Q_LORA + KV_LORA + ROPE
    o4, o5 = o3 + SB_WIDTH, o3 + 2 * SB_WIDTH
    kr_slot = jnp.concatenate([jnp.zeros((D_MODEL, NOPE), w_in.dtype), w_in[:, o2:o3],
                               jnp.zeros((D_MODEL, HEAD_PAD - NOPE - ROPE), w_in.dtype)], axis=1)
    w_in_r = jnp.concatenate([w_in[:, :o2], w_in[:, o3:o4], w_in[:, o4:o5], w_in[:, o5:], kr_slot], axis=1)
    w_q_slot = jnp.concatenate([w_q_up, jnp.zeros((Q_LORA, N_HEADS, HEAD_PAD - NOPE - ROPE), w_q_up.dtype)],
                               axis=2).reshape(Q_LORA, QK_WIDTH)
    w_uk = w_kv_up[:, :, :NOPE]
    w_uv = w_kv_up[:, :, NOPE:]
    w_ukp = jnp.concatenate([w_uk, jnp.zeros_like(w_uk)], axis=2).reshape(KV_LORA, QK_WIDTH)
    w_ukt = jnp.concatenate([jnp.transpose(w_uk, (1, 2, 0)),
                             jnp.zeros((N_HEADS, HEAD_PAD - NOPE, KV_LORA), w_uk.dtype)], axis=1)
    r = jnp.arange(QK_WIDTH)
    c = jnp.arange(N_HEADS * ROPE)
    sel = ((r[:, None] // HEAD_PAD == c[None, :] // ROPE)
           & (r[:, None] % HEAD_PAD == NOPE + c[None, :] % ROPE)).astype(bf16)
    return dict(
        g_attn=g_attn_norm.reshape(1, D_MODEL), w_in=w_in_r.astype(bf16), g_q=g_q_norm.reshape(1, Q_LORA),
        w_q=w_q_slot.astype(bf16), g_kv=g_kv_norm.reshape(1, KV_LORA), w_ukp=w_ukp.astype(bf16),
        w_uvf=w_uv.reshape(KV_LORA, MLA_WIDTH).astype(bf16),
        w_uvt=w_uv.reshape(KV_LORA, MLA_WIDTH).T.astype(bf16), w_ukt=w_ukt.astype(bf16), sel=sel)


def _store_heads(o_ref, outs_t):
    for pair in range(N_HEADS // 2):
        both = jnp.concatenate([outs_t[2 * pair], outs_t[2 * pair + 1]], axis=0)
        o_ref[:, pair * LANES:(pair + 1) * LANES] = both.T


def _mla_prompt_kernel(qt_ref, k_ref, vt_ref, o_ref, *, tq):
    qi = pl.program_id(1)
    f32, bf16 = jnp.float32, jnp.bfloat16
    key = lax.broadcasted_iota(jnp.int32, (tq, tq), 0)
    qry = lax.broadcasted_iota(jnp.int32, (tq, tq), 1)
    causal = key <= qry

    def scores(ks, h):
        k = k_ref[pl.ds(ks, tq), h * HEAD_PAD:(h + 1) * HEAD_PAD]
        return _dot(k, qt_ref[h * HEAD_PAD:(h + 1) * HEAD_PAD, :])

    def step(j, carry, masked):
        ks = pl.multiple_of(j * tq, tq)
        new = [None] * N_HEADS
        ahead = [scores(ks, 0), scores(ks, 1)]
        pending = None
        for h in range(N_HEADS):
            m, l, acc = carry[h]
            s = ahead.pop(0)
            if h + 2 < N_HEADS:
                ahead.append(scores(ks, h + 2))
            if masked:
                s = jnp.where(causal, s, NEG_BIG)
            m_new = jnp.maximum(m, jnp.max(s, axis=0, keepdims=True))
            p = jnp.exp2(s - m_new)
            alpha = jnp.exp2(m - m_new)
            l = alpha * l + jnp.sum(p, axis=0, keepdims=True)
            pv = _dot(vt_ref[j, h * HEAD_DIM:(h + 1) * HEAD_DIM, :], p.astype(bf16))
            if pending is not None:
                g, m_g, l_g, alpha_g, acc_g, pv_g = pending
                new[g] = (m_g, l_g, alpha_g * acc_g + pv_g)
            pending = (h, m_new, l, alpha, acc, pv)
        g, m_g, l_g, alpha_g, acc_g, pv_g = pending
        new[g] = (m_g, l_g, alpha_g * acc_g + pv_g)
        return tuple(new)

    init = tuple((jnp.full((1, tq), NEG_BIG, f32), jnp.zeros((1, tq), f32), jnp.zeros((HEAD_DIM, tq), f32))
                 for _ in range(N_HEADS))
    def two_steps(i, carry):
        return step(2 * i + 1, step(2 * i, carry, False), False)

    carry = lax.fori_loop(0, qi // 2, two_steps, init)
    carry = lax.cond(qi % 2 == 1, lambda c: step(qi - 1, c, False), lambda c: c, carry)
    carry = step(qi, carry, True)
    _store_heads(o_ref, [acc / l for (_, l, acc) in carry])


def _sb_prompt_kernel(sqt_ref, k_ref, vt_ref, o_ref, *, tq):
    qi = pl.program_id(1)
    f32, bf16 = jnp.float32, jnp.bfloat16
    key = lax.broadcasted_iota(jnp.int32, (tq, tq), 0)
    qry = lax.broadcasted_iota(jnp.int32, (tq, tq), 1)
    strict = key < qry
    later = (qry > key).astype(bf16)
    zeros = jnp.zeros((HEAD_DIM, tq), bf16)

    def q_slot(h):
        qh = sqt_ref[h * HEAD_DIM:(h + 1) * HEAD_DIM, :]
        return jnp.concatenate([qh, zeros] if h % 2 == 0 else [zeros, qh], axis=0)

    def logits(ks, h):
        k2 = k_ref[pl.ds(ks, tq), (h // 2) * LANES:(h // 2 + 1) * LANES]
        return _dot(k2, q_slot(h))

    def step(j, carry, masked):
        ks = pl.multiple_of(j * tq, tq)
        new = [None] * N_HEADS

        def stage2(h, z, sp, cum):
            r, acc = carry[h]
            a = jnp.exp(z - sp - cum - r)
            if masked:
                a = jnp.where(strict, a, 0.0)
            pv = _dot(vt_ref[j, h * HEAD_DIM:(h + 1) * HEAD_DIM, :], a.astype(bf16))
            return h, r + jnp.sum(sp, axis=0, keepdims=True), acc, pv

        z_next = logits(ks, 0)
        mid = None
        tail = None
        for h in range(N_HEADS):
            z = z_next
            if h + 1 < N_HEADS:
                z_next = logits(ks, h + 1)
            sp = jnp.maximum(z, 0.0) + jnp.log(1.0 + jnp.exp(-jnp.abs(z)))
            if masked:
                sp = jnp.where(strict, sp, 0.0)
            hi, lo = _split_bf16(sp)
            cum = _dot(later, hi) + _dot(later, lo)
            if mid is not None:
                done = stage2(*mid)
                if tail is not None:
                    g, r_g, acc_g, pv_g = tail
                    new[g] = (r_g, acc_g + pv_g)
                tail = done
            mid = (h, z, sp, cum)
        done = stage2(*mid)
        for g, r_g, acc_g, pv_g in (tail, done):
            new[g] = (r_g, acc_g + pv_g)
        return tuple(new)

    def all_dead(carry):
        lowest = carry[0][0]
        for h in range(1, N_HEADS):
            lowest = jnp.minimum(lowest, carry[h][0])
        return jnp.min(lowest) > SB_DEAD

    init = tuple((jnp.zeros((1, tq), f32), jnp.zeros((HEAD_DIM, tq), f32)) for _ in range(N_HEADS))
    carry = step(qi, init, True)

    def cond(state):
        i, dead, _ = state
        return jnp.logical_and(i < qi, jnp.logical_not(dead))

    def body(state):
        i, _, carry = state
        carry = step(qi - 1 - i, carry, False)
        return i + 1, all_dead(carry), carry

    _, _, carry = lax.while_loop(cond, body, (jnp.int32(0), all_dead(carry), carry))
    _store_heads(o_ref, [acc for (_, acc) in carry])


def _prompt_attn_call(kern, qt, k, vt, *, batch, seq, tq, name):
    nq = seq // tq
    return pl.pallas_call(
        functools.partial(kern, tq=tq),
        out_shape=jax.ShapeDtypeStruct((batch * seq, N_HEADS * HEAD_DIM), jnp.float32),
        grid=(batch, nq),
        in_specs=[pl.BlockSpec((None, qt.shape[1], tq), lambda b, i: (b * nq + i, 0, 0)),
                  pl.BlockSpec((seq, k.shape[1]), lambda b, i: (b, 0)),
                  pl.BlockSpec((nq, vt.shape[1], tq), lambda b, i: (b, 0, 0))],
        out_specs=pl.BlockSpec((tq, N_HEADS * HEAD_DIM), lambda b, i: (b * nq + i, 0)),
        compiler_params=_cparams(("parallel", "arbitrary")),
        name=name,
    )(qt, k, vt)


N_QH = 64


def _softplus(z):
    return jnp.maximum(z, 0.0) + jnp.log1p(jnp.exp(-jnp.abs(z)))


def _split_bf16(x):
    hi = x.astype(jnp.bfloat16)
    return hi, (x - hi.astype(jnp.float32)).astype(jnp.bfloat16)


def _own_head(full):
    rowi = lax.broadcasted_iota(jnp.int32, full.shape, 0)
    lanei = lax.broadcasted_iota(jnp.int32, full.shape, 1)
    kept = jnp.where(lanei // HEAD_DIM == rowi % N_HEADS, full, 0.0)
    return jnp.sum(kept.reshape(N_QH // N_HEADS, N_HEADS, full.shape[1]), axis=1)


def _pad_rows(x, rows):
    return jnp.concatenate([x, jnp.zeros((rows - x.shape[0], x.shape[1]), x.dtype)], axis=0)


SB_SUB = 2


MLA_SLOTS = 4
MLA_AHEAD = MLA_SLOTS - 1


def _sample_attn_kernel(pt_ref, qabs_ref, qrope_ref, sq_ref, cn_ref, krn_ref, skn_ref, svn_ref, wuv_ref,
                        ckv_hbm, kr_hbm, sbk_hbm, sbv_hbm, o_mla_ref, o_sb_ref,
                        r_sc, accsb_sc, qbd_sc, kbuf, vbuf, sems, ckbuf, krbuf, ckb_sc, msems, *, n_pg, n_pages):
    seq = pl.program_id(0)
    n_seq = pl.num_programs(0)
    n_sub = n_pages // SB_SUB
    n_chunks = n_pages // n_pg
    bf16 = jnp.bfloat16

    def mla_copies(g):
        s_idx, chunk, slot = g // n_chunks, g % n_chunks, g % MLA_SLOTS
        out = []
        for p in range(n_pg):
            page = pt_ref[s_idx, n_pages - 1 - (chunk * n_pg + p)]
            out.append(pltpu.make_async_copy(ckv_hbm.at[page], ckbuf.at[slot, p], msems.at[slot]))
            out.append(pltpu.make_async_copy(kr_hbm.at[page], krbuf.at[slot, :, pl.ds(p * PAGE, PAGE)],
                                             msems.at[slot]))
        return out

    def sb_copies(s_idx, sub, slot):
        out = []
        for p in range(SB_SUB):
            page = pt_ref[s_idx, n_pages - 1 - (sub * SB_SUB + p)]
            out.append(pltpu.make_async_copy(sbk_hbm.at[page], kbuf.at[slot, p], sems.at[slot]))
            out.append(pltpu.make_async_copy(sbv_hbm.at[page], vbuf.at[slot, p], sems.at[slot]))
        return out
    qabs = qabs_ref[...]
    qrope = qrope_ref[...]

    later = (lax.broadcasted_iota(jnp.int32, (PAGE, PAGE), 0)
             > lax.broadcasted_iota(jnp.int32, (PAGE, PAGE), 1)).astype(bf16)

    def suffix_in_block(lg):
        hi, lo = _split_bf16(lg)
        both = _dot(jnp.concatenate([hi, lo], axis=0), later)
        return both[:N_QH] + both[N_QH:]

    g0 = seq * n_chunks
    n_total = n_seq * n_chunks

    @pl.when(seq == 0)
    def _():
        for g in range(MLA_AHEAD):
            for cp in mla_copies(g):
                cp.start()

    key = lax.broadcasted_iota(jnp.int32, (N_QH, PAGE), 1)
    qry = lax.broadcasted_iota(jnp.int32, (N_QH, PAGE), 0) // N_HEADS

    def new_rows_latent():
        cn = _pad_rows(cn_ref[...], PAGE).astype(bf16)
        krn = _pad_rows(krn_ref[...], PAGE).astype(bf16)
        s = _dot_nt(qabs, cn) + _dot_nt(qrope, krn)
        s = jnp.where(key <= qry, s, NEG_BIG)
        m = jnp.max(s, axis=1, keepdims=True)
        p = jnp.exp2(s - m)
        return m, jnp.sum(p, axis=1, keepdims=True), _dot(p.astype(bf16), cn)

    def new_rows_and_stick_breaking():
        sq16 = _pad_rows(sq_ref[...], 16).astype(bf16)
        pick = (lax.broadcasted_iota(jnp.int32, (N_QH, 16), 0) // N_HEADS
                == lax.broadcasted_iota(jnp.int32, (N_QH, 16), 1)).astype(bf16)
        rep = _dot(pick, sq16)
        rowi = lax.broadcasted_iota(jnp.int32, rep.shape, 0)
        lanei = lax.broadcasted_iota(jnp.int32, rep.shape, 1)
        qbd = jnp.where(lanei // HEAD_DIM == rowi % N_HEADS, rep, 0.0).astype(bf16)
        qbd_sc[...] = qbd
        latent0 = new_rows_latent()

        strict = key < qry
        z = _dot_nt(qbd, _pad_rows(skn_ref[...], PAGE).astype(bf16))
        lg = jnp.where(strict, -_softplus(z), 0.0)
        a = jnp.where(strict, jnp.exp(z + lg + suffix_in_block(lg)), 0.0)
        accsb_sc[...] = _dot(a.astype(bf16), _pad_rows(svn_ref[...], PAGE).astype(bf16))
        r_sc[...] = jnp.sum(lg, axis=1, keepdims=True)

        @pl.when(seq == 0)
        def _():
            for cp in sb_copies(0, 0, 0):
                cp.start()

        def dead_now():
            return jnp.max(r_sc[...]) < -SB_DEAD

        def sb_round(state):
            i, _ = state
            slot = i % 2
            for cp in sb_copies(seq, i, slot):
                cp.wait()

            @pl.when(i + 1 < n_sub)
            def _():
                for cp in sb_copies(seq, i + 1, 1 - slot):
                    cp.start()

            r = r_sc[...]
            sb = jnp.zeros((N_QH, SB_WIDTH), jnp.float32)
            for p in range(SB_SUB):
                z = _dot(qbd, kbuf[slot, p].astype(bf16))
                lg = -_softplus(z)
                a = jnp.exp(z + lg + suffix_in_block(lg) + r)
                sb = sb + _dot_nt(a.astype(bf16), vbuf[slot, p].astype(bf16))
                r = r + jnp.sum(lg, axis=1, keepdims=True)
            r_sc[...] = r
            accsb_sc[...] = accsb_sc[...] + sb
            return i + 1, dead_now()

        n_done, _ = lax.while_loop(lambda st: jnp.logical_and(st[0] < n_sub, jnp.logical_not(st[1])),
                                   sb_round, (jnp.int32(0), dead_now()))

        @pl.when(n_done < n_sub)
        def _():
            for cp in sb_copies(seq, n_done, n_done % 2):
                cp.wait()

        @pl.when(seq + 1 < n_seq)
        def _():
            for cp in sb_copies(seq + 1, 0, 0):
                cp.start()

        return latent0

    m, l, acc = new_rows_and_stick_breaking()

    def scores(c):
        slot = (g0 + c) % MLA_SLOTS
        for cp in mla_copies(g0 + c):
            cp.wait()
        ckb_sc[c % 2] = ckbuf[slot].reshape(n_pg * PAGE, KV_LORA).astype(bf16)
        part = n_pg * PAGE // 4
        latent = jnp.concatenate([_dot_nt(qabs, ckb_sc[c % 2, q * part:(q + 1) * part]) for q in range(4)], axis=1)
        return latent + _dot(qrope, krbuf[slot].astype(bf16))

    s_next = scores(0)
    for c in range(n_chunks):
        if c + MLA_AHEAD < n_chunks:
            for cp in mla_copies(g0 + c + MLA_AHEAD):
                cp.start()
        else:
            @pl.when(g0 + c + MLA_AHEAD < n_total)
            def _(c=c):
                for cp in mla_copies(g0 + c + MLA_AHEAD):
                    cp.start()
        s = s_next
        if c + 1 < n_chunks:
            s_next = scores(c + 1)
        m_new = jnp.maximum(m, jnp.max(s, axis=1, keepdims=True))
        alpha = jnp.exp2(m - m_new)
        e = jnp.exp2(s - m_new)
        l = alpha * l + jnp.sum(e, axis=1, keepdims=True)
        acc = acc * alpha + _dot(e.astype(bf16), ckb_sc[c % 2])
        m = m_new

    o_mla_ref[...] = _own_head(_dot((acc / l).astype(bf16), wuv_ref[...]))
    o_sb_ref[...] = _own_head(accsb_sc[...])


def _feature_major(cache):
    return jnp.moveaxis(cache, 1, -1).reshape(cache.shape[0], -1, PAGE)


def _sample_attn_call(page_table, qabs, qrope, sq, ckv_n, kr_n, sk_n, sv_n, w_uvf, cache_ckv, cache_krope,
                      cache_sb_k, cache_sb_v, *, n_pg):
    n_seq, n_pages = page_table.shape
    n_chunks = n_pages // n_pg
    t_new = ckv_n.shape[0] // n_seq

    def seq_blk(shape):
        return pl.BlockSpec((None,) + shape, lambda s, pt: (s, 0, 0))

    def new_blk(w):
        return pl.BlockSpec((t_new, w), lambda s, pt: (s, 0))

    hbm = pl.BlockSpec(memory_space=pl.ANY)
    in_specs = [seq_blk((N_QH, KV_LORA)), seq_blk((N_QH, ROPE)), new_blk(SB_WIDTH), new_blk(KV_LORA),
                new_blk(ROPE), new_blk(SB_WIDTH), new_blk(SB_WIDTH),
                pl.BlockSpec(w_uvf.shape, lambda s, pt: (0, 0)), hbm, hbm, hbm, hbm]
    args = [qabs.reshape(n_seq, N_QH, KV_LORA), qrope.reshape(n_seq, N_QH, ROPE), sq, ckv_n, kr_n, sk_n, sv_n, w_uvf,
            cache_ckv, cache_krope, cache_sb_k, cache_sb_v]
    out_blk = pl.BlockSpec((t_new, MLA_WIDTH), lambda s, pt: (s, 0))
    f32, bf16 = jnp.float32, jnp.bfloat16
    assert n_pages % n_pg == 0 and n_pages % SB_SUB == 0 and n_seq * n_chunks >= MLA_AHEAD
    return pl.pallas_call(
        functools.partial(_sample_attn_kernel, n_pg=n_pg, n_pages=n_pages),
        out_shape=[jax.ShapeDtypeStruct((n_seq * t_new, MLA_WIDTH), f32),
                   jax.ShapeDtypeStruct((n_seq * t_new, SB_WIDTH), f32)],
        grid_spec=pltpu.PrefetchScalarGridSpec(
            num_scalar_prefetch=1, grid=(n_seq,), in_specs=in_specs, out_specs=[out_blk, out_blk],
            scratch_shapes=[pltpu.VMEM((N_QH, 1), f32), pltpu.VMEM((N_QH, SB_WIDTH), f32),
                            pltpu.VMEM((N_QH, SB_WIDTH), bf16),
                            pltpu.VMEM((2, SB_SUB, SB_WIDTH, PAGE), f32), pltpu.VMEM((2, SB_SUB, SB_WIDTH, PAGE), f32),
                            pltpu.SemaphoreType.DMA((2,)),
                            pltpu.VMEM((MLA_SLOTS, n_pg, PAGE, KV_LORA), f32),
                            pltpu.VMEM((MLA_SLOTS, ROPE, n_pg * PAGE), f32),
                            pltpu.VMEM((2, n_pg * PAGE, KV_LORA), bf16),
                            pltpu.SemaphoreType.DMA((MLA_SLOTS,))]),
        compiler_params=_cparams(("arbitrary",)),
        name="sample_attn",
    )(page_table, *args)


def _out_router_kernel(omla_ref, osb_ref, x_ref, gm_ref, gs_ref, wout_ref, gmoe_ref, wr_hi_ref, wr_lo_ref, br_ref,
                       h_ref, tok_ref, eidx_ref, gate_ref, rank_ref, cnt_ref, carry_sc, *, tm):
    bf16 = jnp.bfloat16

    @pl.when(pl.program_id(0) == 0)
    def _():
        carry_sc[...] = jnp.zeros_like(carry_sc)

    o = jnp.concatenate([_rms(omla_ref[...], gm_ref[...]), _rms(osb_ref[...], gs_ref[...])], axis=1).astype(bf16)
    h = x_ref[...] + _dot(o, wout_ref[...])
    h_ref[...] = h
    tok = _rms(h, gmoe_ref[...])
    tok_ref[...] = tok
    t_hi, t_lo = _split_bf16(tok)
    wr_hi = wr_hi_ref[...]
    logits = _dot(t_hi, wr_hi) + (_dot(t_hi, wr_lo_ref[...]) + _dot(t_lo, wr_hi))
    work = logits.T[:N_EXPERTS] + br_ref[...]
    expert = lax.broadcasted_iota(jnp.int32, (N_EXPERTS, tm), 0).astype(jnp.float32)
    vals, idxs, sels = [], [], []
    for _ in range(TOP_K):
        v = jnp.max(work, axis=0, keepdims=True)
        idx = jnp.min(jnp.where(work == v, expert, float(N_EXPERTS)), axis=0, keepdims=True)
        sel = expert == idx
        work = jnp.where(sel, -jnp.inf, work)
        vals.append(v)
        idxs.append(idx)
        sels.append(sel)
    exps = [jnp.exp(v - vals[0]) for v in vals]
    denom = exps[0] + exps[1] + exps[2] + exps[3]
    gates = [e / denom for e in exps]

    assign = jnp.zeros((N_EXPERTS, tm), jnp.float32)
    for sel in sels:
        assign = jnp.where(sel, 1.0, assign)
    before = (lax.broadcasted_iota(jnp.int32, (tm, tm), 0)
              < lax.broadcasted_iota(jnp.int32, (tm, tm), 1)).astype(bf16)
    prefix = _dot(assign.astype(bf16), before) + carry_sc[...]
    ranks = [jnp.sum(jnp.where(sel, prefix, 0.0), axis=0, keepdims=True) for sel in sels]
    carry = carry_sc[...] + jnp.sum(assign, axis=1, keepdims=True)
    carry_sc[...] = carry
    cnt_ref[...] = carry
    pad = jnp.zeros((8 - TOP_K, tm), jnp.float32)
    eidx_ref[...] = jnp.concatenate(idxs + [pad], axis=0)
    gate_ref[...] = jnp.concatenate(gates + [pad], axis=0)
    rank_ref[...] = jnp.concatenate(ranks + [pad], axis=0)


def _router_weights(w_router, b_router):
    w_p = jnp.pad(w_router, ((0, 0), (0, LANES - N_EXPERTS)))
    hi = w_p.astype(jnp.bfloat16)
    return dict(wr_hi=hi, wr_lo=(w_p - hi.astype(jnp.float32)).astype(jnp.bfloat16),
                b_router=b_router.reshape(N_EXPERTS, 1))


def _destinations(row_start, eidx_t, rank_t):
    e = eidx_t[:TOP_K].T.astype(jnp.int32)
    experts = jnp.arange(N_EXPERTS, dtype=jnp.int32)
    start = jnp.sum(jnp.where(e[..., None] == experts, row_start, 0), axis=-1)
    return start + rank_t[:TOP_K].T.astype(jnp.int32)


def _token_major(gate_t):
    return jnp.pad(gate_t[:TOP_K].T, ((0, 0), (0, LANES - TOP_K)))


def _out_router_call(o_mla, o_sb, x2d, wts, *, tm):
    n = x2d.shape[0]
    tok = lambda w: pl.BlockSpec((tm, w), lambda i: (i, 0))
    full = lambda a: pl.BlockSpec(a.shape, lambda i: (0,) * a.ndim)
    f32 = jnp.float32
    sds = jax.ShapeDtypeStruct
    ws = [wts['g_mla_out'], wts['g_sb_out'], wts['w_out'], wts['g_moe'], wts['wr_hi'], wts['wr_lo'], wts['b_router']]
    return pl.pallas_call(
        functools.partial(_out_router_kernel, tm=tm),
        out_shape=[sds((n, D_MODEL), f32), sds((n, D_MODEL), f32), sds((8, n), f32), sds((8, n), f32),
                   sds((8, n), f32), sds((N_EXPERTS, 1), f32)],
        grid=(n // tm,),
        in_specs=[tok(MLA_WIDTH), tok(SB_WIDTH), tok(D_MODEL)] + [full(a) for a in ws],
        out_specs=[tok(D_MODEL), tok(D_MODEL)] + [pl.BlockSpec((8, tm), lambda i: (0, i))] * 3
        + [pl.BlockSpec((N_EXPERTS, 1), lambda i: (0, 0))],
        scratch_shapes=[pltpu.VMEM((N_EXPERTS, 1), f32)],
        compiler_params=_cparams(("arbitrary",)),
        name="out_router",
    )(o_mla, o_sb, x2d, *ws)


def _row_copy(src, s, dst, d, sem):
    return pltpu.make_async_copy(src.at[pl.ds(s, 1), :], dst.at[pl.ds(d, 1), :], sem)


def _dispatch_kernel(dest_ref, tokp_ref, toks_ref, xs_ref, sem, *, tm, n_prompt_tiles):
    i = pl.program_id(0)

    def scatter(tok_ref):
        def start(t, _):
            for k in range(TOP_K):
                _row_copy(tok_ref, t, xs_ref, dest_ref[0, t * TOP_K + k], sem).start()
            return 0

        def wait(t, _):
            for k in range(TOP_K):
                _row_copy(tok_ref, 0, xs_ref, 0, sem).wait()
            return 0

        lax.fori_loop(0, tm, start, 0)
        lax.fori_loop(0, tm, wait, 0)

    @pl.when(i < n_prompt_tiles)
    def _():
        scatter(tokp_ref)

    @pl.when(i >= n_prompt_tiles)
    def _():
        scatter(toks_ref)


def _dispatch_call(dest_tiles, tok_p, tok_s, *, tm):
    npt, nst = tok_p.shape[0] // tm, tok_s.shape[0] // tm
    n_pairs = (tok_p.shape[0] + tok_s.shape[0]) * TOP_K
    return pl.pallas_call(
        functools.partial(_dispatch_kernel, tm=tm, n_prompt_tiles=npt),
        out_shape=jax.ShapeDtypeStruct((n_pairs, D_MODEL), jnp.float32),
        grid=(npt + nst,),
        in_specs=[pl.BlockSpec((None, 1, tm * TOP_K), lambda i: (i, 0, 0), memory_space=pltpu.SMEM),
                  pl.BlockSpec((tm, D_MODEL), lambda i: (jnp.minimum(i, npt - 1), 0)),
                  pl.BlockSpec((tm, D_MODEL), lambda i: (jnp.maximum(i - npt, 0), 0))],
        out_specs=pl.BlockSpec(memory_space=pl.ANY),
        scratch_shapes=[pltpu.SemaphoreType.DMA],
        compiler_params=_cparams(("arbitrary",)),
        name="moe_dispatch",
    )(dest_tiles, tok_p, tok_s)


EXPERT_SUB = 512


def _expert_kernel(blk_ref, exp_ref, lo_ref, hi_ref, first_ref, xs_ref, wup_ref, bup_ref, wdn_ref, bdn_ref,
                   y_ref, wup_sc, wdn_sc, *, bm):
    i = pl.program_id(0)
    bf16 = jnp.bfloat16
    lo, hi = lo_ref[i], hi_ref[i]
    e_now = exp_ref[i]
    e_prev = exp_ref[jnp.maximum(i - 1, 0)]

    @pl.when((i == 0) | (e_now != e_prev))
    def _():
        wup_sc[...] = wup_ref[...].astype(bf16)
        wdn_sc[...] = wdn_ref[...].astype(bf16)

    first = first_ref[i] == 1
    for r0 in range(0, bm, EXPERT_SUB):
        rs = slice(r0, r0 + EXPERT_SUB)
        touched = (lo < r0 + EXPERT_SUB) & (hi > r0)

        @pl.when(touched)
        def _(r0=r0, rs=rs):
            x = xs_ref[rs, :].astype(bf16)
            hcat = _dot(x, wup_sc[...]) + bup_ref[...]
            x_glu = jnp.minimum(hcat[:, :D_FF], SWIGLU_LIMIT)
            x_lin = jnp.clip(hcat[:, D_FF:], -SWIGLU_LIMIT, SWIGLU_LIMIT)
            act = x_glu * jax.nn.sigmoid(SWIGLU_ALPHA * x_glu) * (x_lin + 1.0)
            y = _dot(act.astype(bf16), wdn_sc[...]) + bdn_ref[...]
            rows = r0 + lax.broadcasted_iota(jnp.int32, (EXPERT_SUB, 1), 0)
            y = jnp.where((rows >= lo) & (rows < hi), y, 0.0)

            @pl.when(first)
            def _():
                y_ref[rs, :] = y

            @pl.when(jnp.logical_not(first))
            def _():
                y_ref[rs, :] = y_ref[rs, :] + y

        @pl.when(first & jnp.logical_not(touched))
        def _(rs=rs):
            y_ref[rs, :] = jnp.zeros((EXPERT_SUB, D_MODEL), jnp.float32)


def _expert_call(items, xs, w_up, b_up, w_dn, b_dn, *, bm):
    n_items = items[0].shape[0]
    return pl.pallas_call(
        functools.partial(_expert_kernel, bm=bm),
        out_shape=jax.ShapeDtypeStruct(xs.shape, jnp.float32),
        grid_spec=pltpu.PrefetchScalarGridSpec(
            num_scalar_prefetch=5, grid=(n_items,),
            in_specs=[pl.BlockSpec((bm, D_MODEL), lambda i, blk, ex, lo, hi, fi: (blk[i], 0)),
                      pl.BlockSpec((None, D_MODEL, 2 * D_FF), lambda i, blk, ex, lo, hi, fi: (ex[i], 0, 0)),
                      pl.BlockSpec((None, 1, 2 * D_FF), lambda i, blk, ex, lo, hi, fi: (ex[i], 0, 0)),
                      pl.BlockSpec((None, D_FF, D_MODEL), lambda i, blk, ex, lo, hi, fi: (ex[i], 0, 0)),
                      pl.BlockSpec((None, 1, D_MODEL), lambda i, blk, ex, lo, hi, fi: (ex[i], 0, 0))],
            out_specs=pl.BlockSpec((bm, D_MODEL), lambda i, blk, ex, lo, hi, fi: (blk[i], 0)),
            scratch_shapes=[pltpu.VMEM((D_MODEL, 2 * D_FF), jnp.bfloat16), pltpu.VMEM((D_FF, D_MODEL), jnp.bfloat16)]),
        compiler_params=_cparams(("arbitrary",)),
        name="moe_experts",
    )(*items, xs, w_up, b_up.reshape(N_EXPERTS, 1, 2 * D_FF), w_dn, b_dn.reshape(N_EXPERTS, 1, D_MODEL))


def _combine_kernel(dest_ref, dest_next_ref, h_ref, gate_ref, gf_ref, ys_ref, out_ref, buf, sems, *, tm):
    i = pl.program_id(0)
    slot = i % 2

    def gather(d_ref, sl):
        def start(t, _):
            for k in range(TOP_K):
                pltpu.make_async_copy(ys_ref.at[pl.ds(d_ref[0, t * TOP_K + k], 1), :],
                                      buf.at[sl, k, pl.ds(t, 1), :], sems.at[sl]).start()
            return 0
        lax.fori_loop(0, tm, start, 0)

    @pl.when(i == 0)
    def _():
        gather(dest_ref, 0)

    @pl.when(i + 1 < pl.num_programs(0))
    def _():
        gather(dest_next_ref, 1 - slot)

    def wait(t, _):
        for k in range(TOP_K):
            pltpu.make_async_copy(ys_ref.at[pl.ds(0, 1), :], buf.at[slot, 0, pl.ds(0, 1), :], sems.at[slot]).wait()
        return 0

    lax.fori_loop(0, tm, wait, 0)
    gate = gate_ref[...]
    h = h_ref[...]
    for k in range(TOP_K):
        h = h + gate[:, k:k + 1] * buf[slot, k]
    out_ref[...] = _rms(h, gf_ref[...])


def _combine_call(dest_tiles, h, gate, g_final, ys, *, tm):
    n = h.shape[0]
    nt = n // tm
    dest_blk = lambda f: pl.BlockSpec((None, 1, tm * TOP_K), lambda i: (f(i), 0, 0), memory_space=pltpu.SMEM)
    return pl.pallas_call(
        functools.partial(_combine_kernel, tm=tm),
        out_shape=jax.ShapeDtypeStruct((n, D_MODEL), jnp.float32),
        grid=(nt,),
        in_specs=[dest_blk(lambda i: i), dest_blk(lambda i: jnp.minimum(i + 1, nt - 1)),
                  pl.BlockSpec((tm, D_MODEL), lambda i: (i, 0)),
                  pl.BlockSpec((tm, LANES), lambda i: (i, 0)),
                  pl.BlockSpec((1, D_MODEL), lambda i: (0, 0)),
                  pl.BlockSpec(memory_space=pl.ANY)],
        out_specs=pl.BlockSpec((tm, D_MODEL), lambda i: (i, 0)),
        scratch_shapes=[pltpu.VMEM((2, TOP_K, tm, D_MODEL), jnp.float32), pltpu.SemaphoreType.DMA((2,))],
        compiler_params=_cparams(("arbitrary",)),
        name="moe_combine",
    )(dest_tiles, dest_tiles, h, gate, g_final.reshape(1, D_MODEL), ys)


def _plan_items(counts, n_pairs, bm):
    n_blocks = n_pairs // bm
    n_items = n_blocks + N_EXPERTS - 1
    experts = jnp.arange(N_EXPERTS, dtype=jnp.int32)
    ends = jnp.cumsum(counts)
    starts = ends - counts
    first_blk = starts // bm
    last_blk = jnp.where(counts > 0, (ends - 1) // bm, first_blk - 1)
    n_touch = jnp.maximum(last_blk - first_blk + 1, 0)
    item_end = jnp.cumsum(n_touch)
    item_start = item_end - n_touch
    it = jnp.arange(n_items, dtype=jnp.int32)
    valid = it < item_end[-1]
    last_e = jnp.max(jnp.where(counts > 0, experts, 0))
    e_of = jnp.where(valid, jnp.sum(it[:, None] >= item_end[None, :], axis=1), last_e).astype(jnp.int32)
    mine = e_of[:, None] == experts[None, :]
    pick = lambda v: jnp.sum(jnp.where(mine, v[None, :], 0), axis=1)
    blk_of = jnp.where(valid, pick(first_blk) + it - pick(item_start), n_blocks - 1).astype(jnp.int32)
    lo = jnp.where(valid, jnp.clip(pick(starts) - blk_of * bm, 0, bm), 0).astype(jnp.int32)
    hi = jnp.where(valid, jnp.clip(pick(ends) - blk_of * bm, 0, bm), 0).astype(jnp.int32)
    prev_blk = jnp.concatenate([jnp.full((1,), -1, jnp.int32), blk_of[:-1]])
    first = (valid & (blk_of != prev_blk)).astype(jnp.int32)
    return blk_of, e_of.astype(jnp.int32), lo, hi, first


def kernel(x_prompt, x_sample, cache_ckv, cache_krope, cache_sb_k, cache_sb_v, page_table, g_attn_norm, w_in,
           g_q_norm, w_q_up, g_kv_norm, w_kv_up, g_mla_out, g_sb_out, w_out, g_moe_norm, w_router, b_router,
           w_moe_up, b_moe_up, w_moe_down, b_moe_down, g_final):
    b_p, s_p, d = x_prompt.shape
    b_s, s_s, _ = x_sample.shape
    n_p, n_s = b_p * s_p, b_s * s_s
    n_pool = cache_ckv.shape[1]
    past_len = page_table.shape[1] * PAGE
    bf16 = jnp.bfloat16
    wts = _prep_weights(g_attn_norm[0], w_in[0], g_q_norm[0], w_q_up[0], g_kv_norm[0], w_kv_up[0])
    wts.update(_router_weights(w_router[0], b_router[0]))
    wts.update(
        g_mla_out=g_mla_out[0].reshape(1, MLA_WIDTH), g_sb_out=g_sb_out[0].reshape(1, SB_WIDTH),
        w_out=w_out[0].astype(bf16), g_moe=g_moe_norm[0].reshape(1, D_MODEL))
    x_p2 = x_prompt.reshape(n_p, d)
    x_s2 = x_sample.reshape(n_s, d)
    tm = 256

    cos_p, sin_p = _rope_tables(jnp.arange(s_p))
    (ckv_p, krt_p, skt_p, svt_p, qt_p, k_p, vt_p, sqt_p, skb_p, svtb_p) = _proj_call(
        x_p2, cos_p, sin_p, wts, sample=False, seq_tiles=s_p // tm, tm=tm)
    o_mla_p = _prompt_attn_call(_mla_prompt_kernel, qt_p, k_p, vt_p, batch=b_p, seq=s_p, tq=tm, name="mla_prompt")
    o_sb_p = _prompt_attn_call(_sb_prompt_kernel, sqt_p, skb_p, svtb_p, batch=b_p, seq=s_p, tq=tm, name="sb_prompt")
    h_p, tok_p, eidx_p, gate_p, rank_p, cnt_p = _out_router_call(o_mla_p, o_sb_p, x_p2, wts, tm=tm)

    cos_s, sin_s = _rope_tables(past_len + jnp.arange(s_s))
    cos_s, sin_s = jnp.tile(cos_s, (b_s, 1)), jnp.tile(sin_s, (b_s, 1))
    (ckv_s, kr_s, sk_s, sv_s, sq_s, qabs_s, qrope_s) = _proj_call(
        x_s2, cos_s, sin_s, wts, sample=True, seq_tiles=n_s // tm, tm=tm)
    o_mla_s, o_sb_s = _sample_attn_call(
        page_table, qabs_s, qrope_s, sq_s.astype(jnp.float32), ckv_s, kr_s, sk_s, sv_s, wts['w_uvf'],
        cache_ckv[0], _feature_major(cache_krope[0]), _feature_major(cache_sb_k[0]), _feature_major(cache_sb_v[0]),
        n_pg=16)
    h_s, tok_s, eidx_s, gate_s, rank_s, cnt_s = _out_router_call(o_mla_s, o_sb_s, x_s2, wts, tm=tm)

    cnt_p = cnt_p[:, 0].astype(jnp.int32)
    cnt_s = cnt_s[:, 0].astype(jnp.int32)
    counts = cnt_p + cnt_s
    pstart = jnp.cumsum(counts) - counts
    dest_p = _destinations(pstart, eidx_p, rank_p).reshape(n_p // tm, 1, tm * TOP_K)
    dest_s = _destinations(pstart + cnt_p, eidx_s, rank_s).reshape(n_s // tm, 1, tm * TOP_K)
    gate_p, gate_s = _token_major(gate_p), _token_major(gate_s)
    bm = 512
    items = _plan_items(counts, (n_p + n_s) * TOP_K, bm)

    xs = _dispatch_call(jnp.concatenate([dest_p, dest_s], axis=0), tok_p, tok_s, tm=tm)
    ys = _expert_call(items, xs, w_moe_up[0], b_moe_up[0], w_moe_down[0], b_moe_down[0], bm=bm)
    y_p = _combine_call(dest_p, h_p, gate_p, g_final, ys, tm=tm)
    y_s = _combine_call(dest_s, h_s, gate_s, g_final, ys, tm=tm)

    return (y_p.reshape(b_p, s_p, d), y_s.reshape(b_s, s_s, d),
            ckv_p.reshape(1, b_p, s_p, KV_LORA), jnp.swapaxes(krt_p, 1, 2)[None],
            jnp.moveaxis(skt_p.reshape(b_p, N_HEADS, HEAD_DIM, s_p), 3, 1)[None],
            jnp.moveaxis(svt_p.reshape(b_p, N_HEADS, HEAD_DIM, s_p), 3, 1)[None],
            ckv_s.reshape(1, b_s, s_s, KV_LORA), kr_s.reshape(1, b_s, s_s, ROPE),
            sk_s.reshape(1, b_s, s_s, N_HEADS, HEAD_DIM), sv_s.reshape(1, b_s, s_s, N_HEADS, HEAD_DIM))
```
